```python
import jax, jax.numpy as jnp
from jax import lax
import numpy as np

D_MODEL = 1024
BATCH = 2
SEQ = 8192
DEPTH = 1

POOL_WIDTH = D_MODEL // 2
POOL_GROUPS = 4
POOL_GROUP_DIM = POOL_WIDTH // POOL_GROUPS
POOL_WINDOWS = (2, 4, 8, 16)
N_HEADS = 8
HEAD_DIM = 64
ATTN_WIDTH = N_HEADS * HEAD_DIM
ROT_DIM = HEAD_DIM // 4
ROPE_THETA = 500000.0
IDX_HEADS = 8
IDX_DIM = 64
TOPK_MAX = 256
Q_BLOCK = 128
N_BRANCH = 2
D_FF = 2816
CONV_WIDTH = 3
EPS = 1e-6
IN_SPLITS = (POOL_WIDTH, ATTN_WIDTH, ATTN_WIDTH, ATTN_WIDTH,
             IDX_HEADS * IDX_DIM, IDX_DIM, IDX_HEADS, N_BRANCH * D_MODEL)
D_IN = POOL_WIDTH + 3 * ATTN_WIDTH + IDX_HEADS * IDX_DIM + IDX_DIM + IDX_HEADS + N_BRANCH * D_MODEL

kernel_name = "hybrid_pool_dsa_gated_convffn"


def rmsnorm(x, g):
    xf = x.astype(jnp.float32)
    y = xf * lax.rsqrt(jnp.mean(xf * xf, axis=-1, keepdims=True) + EPS)
    return (y * g.astype(jnp.float32)).astype(x.dtype)


def rope_partial(x, pos):
    half = ROT_DIM // 2
    inv_freq = 1.0 / (ROPE_THETA ** (jnp.arange(half, dtype=jnp.float32) * 2.0 / ROT_DIM))
    ang = pos[:, None] * inv_freq[None, :]
    cos = jnp.cos(ang)[None, :, None, :]
    sin = jnp.sin(ang)[None, :, None, :]
    xr = x[..., :ROT_DIM].astype(jnp.float32)
    x1, x2 = xr[..., :half], xr[..., half:]
    rot = jnp.concatenate([x1 * cos - x2 * sin, x2 * cos + x1 * sin], axis=-1)
    return jnp.concatenate([rot.astype(x.dtype), x[..., ROT_DIM:]], axis=-1)


def multiscale_pool(u):
    B, S, _ = u.shape
    ug = u.astype(jnp.float32).reshape(B, S, POOL_GROUPS, POOL_GROUP_DIM)
    c0 = jnp.concatenate([jnp.zeros((B, 1, POOL_GROUPS, POOL_GROUP_DIM), jnp.float32),
                          jnp.cumsum(ug, axis=1)], axis=1)
    t = jnp.arange(S)
    outs = []
    for g, w in enumerate(POOL_WINDOWS):
        lo = jnp.maximum(t + 1 - w, 0)
        cg = c0[:, :, g]
        sums = cg[:, 1:] - cg[:, lo]
        cnt = jnp.minimum(t + 1, w).astype(jnp.float32)[None, :, None]
        outs.append(sums / cnt - ug[:, :, g])
    return jnp.stack(outs, axis=2).astype(u.dtype)


def indexed_sparse_attention(q, k, v, iq, ik, iw):
    B, S, H, Dh = q.shape
    k_top = min(TOPK_MAX, S // 4)
    nblk = S // Q_BLOCK
    kpos = jnp.arange(S)

    def to_blocks(a):
        return a.reshape((B, nblk, Q_BLOCK) + a.shape[2:]).swapaxes(0, 1)

    t_blocks = kpos.reshape(nblk, Q_BLOCK)

    def block_fn(args):
        qb, iqb, iwb, tb = args
        s = jnp.einsum('bqhd,bkd->bhqk', iqb, ik).astype(jnp.float32) * (IDX_DIM ** -0.5)
        w = iwb.astype(jnp.float32) * (IDX_HEADS ** -0.5)
        score = jnp.einsum('bqh,bhqk->bqk', w, jax.nn.relu(s))
        causal = kpos[None, :] <= tb[:, None]
        score = jnp.where(causal[None], score, -jnp.inf)
        _, idx = lax.top_k(score, k_top)
        valid = idx <= tb[None, :, None]
        kg = jax.vmap(lambda kb, ib: kb[ib])(k, idx)
        vg = jax.vmap(lambda vb, ib: vb[ib])(v, idx)
        sc = jnp.einsum('bqhd,bqkhd->bhqk', qb, kg).astype(jnp.float32) * (Dh ** -0.5)
        sc = jnp.where(valid[:, None], sc, -jnp.inf)
        p = jax.nn.softmax(sc, axis=-1)
        return jnp.einsum('bhqk,bqkhd->bqhd', p.astype(vg.dtype), vg)

    out = lax.map(block_fn, (to_blocks(q), to_blocks(iq), to_blocks(iw), t_blocks))
    return out.swapaxes(0, 1).reshape(B, S, H * Dh)


def hybrid_mixer(xn, w_in, pool_w, pool_scale, w_pool_proj, w_attn_proj, w_out):
    B, S, _ = xn.shape
    proj = xn @ w_in
    cuts = [int(c) for c in np.cumsum(IN_SPLITS)[:-1]]
    u_pool, q, k, v, iq, ik, iw, gates = jnp.split(proj, cuts, axis=-1)
    pos = jnp.arange(S, dtype=jnp.float32)

    pooled = multiscale_pool(u_pool)
    mixed = jnp.einsum('bsgc,gcd->bsgd', pooled, pool_w).reshape(B, S, POOL_WIDTH) * pool_scale
    y_pool = mixed @ w_pool_proj

    q = rope_partial(q.reshape(B, S, N_HEADS, HEAD_DIM), pos)
    k = rope_partial(k.reshape(B, S, N_HEADS, HEAD_DIM), pos)
    v = v.reshape(B, S, N_HEADS, HEAD_DIM)
    iq = rope_partial(iq.reshape(B, S, IDX_HEADS, IDX_DIM), pos)
    ik = rope_partial(ik.reshape(B, S, 1, IDX_DIM), pos)[:, :, 0]
    o = indexed_sparse_attention(q, k, v, iq, ik, iw)
    y_attn = o @ w_attn_proj

    g = jax.nn.sigmoid(gates.astype(jnp.float32)).reshape(B, S, N_BRANCH, D_MODEL)
    merged = (g[:, :, 0] * y_pool.astype(jnp.float32) + g[:, :, 1] * y_attn.astype(jnp.float32)).astype(xn.dtype)
    return merged @ w_out


def conv_ffn(xn, w_up, conv_w, conv_b, w_down):
    h = xn @ w_up
    C = h.shape[-1]
    h = lax.conv_general_dilated(h, conv_w[:, None, :].astype(h.dtype), window_strides=(1,),
                                 padding=[(CONV_WIDTH - 1, 0)],
                                 dimension_numbers=('NWC', 'WIO', 'NWC'),
                                 feature_group_count=C) + conv_b
    a, b = jnp.split(h, 2, axis=-1)
    return (jax.nn.silu(a) * b) @ w_down


def setup_inputs(seed: int = 0) -> dict:
    key = jax.random.key(seed)
    ks = jax.random.split(key, 16)
    f32 = jnp.float32
    nrm = lambda k, shape, fan: jax.random.normal(k, shape, f32) * (fan ** -0.5)
    L = DEPTH
    return {
        "x": jax.random.normal(ks[0], (BATCH, SEQ, D_MODEL), f32),
        "norm_mix_g": 1.0 + 0.05 * jax.random.normal(ks[1], (L, D_MODEL), f32),
        "w_in": nrm(ks[2], (L, D_MODEL, D_IN), D_MODEL),
        "pool_w": nrm(ks[3], (L, POOL_GROUPS, POOL_GROUP_DIM, POOL_GROUP_DIM), POOL_GROUP_DIM),
        "pool_scale": 1.0 + 0.1 * jax.random.normal(ks[4], (L, POOL_WIDTH), f32),
        "w_pool_proj": nrm(ks[5], (L, POOL_WIDTH, D_MODEL), POOL_WIDTH),
        "w_attn_proj": nrm(ks[6], (L, ATTN_WIDTH, D_MODEL), ATTN_WIDTH),
        "w_out": nrm(ks[7], (L, D_MODEL, D_MODEL), D_MODEL),
        "norm_ffn_g": 1.0 + 0.05 * jax.random.normal(ks[8], (L, D_MODEL), f32),
        "w_up": nrm(ks[9], (L, D_MODEL, 2 * D_FF), D_MODEL),
        "conv_w": nrm(ks[10], (L, CONV_WIDTH, 2 * D_FF), CONV_WIDTH),
        "conv_b": 0.02 * jax.random.normal(ks[11], (L, 2 * D_FF), f32),
        "w_down": nrm(ks[12], (L, D_FF, D_MODEL), D_FF),
        "norm_final_g": 1.0 + 0.05 * jax.random.normal(ks[13], (D_MODEL,), f32),
    }


def reference(x, norm_mix_g, w_in, pool_w, pool_scale, w_pool_proj, w_attn_proj, w_out,
              norm_ffn_g, w_up, conv_w, conv_b, w_down, norm_final_g):
    h = x
    for l in range(DEPTH):
        h = h + hybrid_mixer(rmsnorm(h, norm_mix_g[l]), w_in[l], pool_w[l], pool_scale[l],
                             w_pool_proj[l], w_attn_proj[l], w_out[l])
        h = h + conv_ffn(rmsnorm(h, norm_ffn_g[l]), w_up[l], conv_w[l], conv_b[l], w_down[l])
    return rmsnorm(h, norm_final_g)
```

```python
import functools
import math

import jax
import jax.numpy as jnp
from jax import lax
from jax.experimental import pallas as pl
from jax.experimental.pallas import tpu as pltpu

POOL_GROUPS = 4
POOL_WINDOWS = (2, 4, 8, 16)
N_HEADS = 8
HEAD_DIM = 64
ROT_DIM = HEAD_DIM // 4
ROPE_THETA = 500000.0
IDX_HEADS = 8
IDX_DIM = 64
TOPK_MAX = 256
N_BRANCH = 2
CONV_WIDTH = 3
EPS = 1e-6

LANES = 128
SUBLANES = 8
VMEM_LIMIT_BYTES = 56 * 1024 * 1024
MXU_DTYPE = jnp.bfloat16

TQ = 256
TK = 256
COUNT_ROWS = 64
POOL_HALO = 16
FFN_HALO = 16
FF_CHUNK = 256

INT_MIN = -(2 ** 31)
NEG_BIG = -1e30

_F32 = jnp.float32


def _rmsnorm(x, g):
    return x * lax.rsqrt(jnp.mean(x * x, axis=-1, keepdims=True) + EPS) * g


def _params(**kw):
    return pltpu.CompilerParams(vmem_limit_bytes=VMEM_LIMIT_BYTES, **kw)


def _resident(block_shape, index_map):
    return pl.BlockSpec(block_shape, index_map, pipeline_mode=pl.Buffered(1))


def _proj_kernel(x_ref, g_ref, wrow_ref, wt_ref, cos_ref, sin_ref, ra_ref, rb_ref, rc_ref,
                 qT_ref, iqT_ref, vT_ref, wT_ref, k_ref, ik_ref, *, n_vt):
    attn_w = N_HEADS * HEAD_DIM
    idx_w = IDX_HEADS * IDX_DIM
    xn = _rmsnorm(x_ref[...], g_ref[...]).astype(MXU_DTYPE)
    row = jnp.dot(xn, wrow_ref[...], preferred_element_type=_F32)
    tr = lax.dot_general(wt_ref[...], xn, (((1,), (1,)), ((), ())),
                         preferred_element_type=_F32)

    cos = cos_ref[...]
    sin = sin_ref[...]
    half = ROT_DIM // 2

    def rope_t(z, n_heads, dim):
        parts = []
        for h in range(n_heads):
            b = h * dim
            x1 = z[b:b + half]
            x2 = z[b + half:b + ROT_DIM]
            parts += [x1 * cos - x2 * sin, x2 * cos + x1 * sin, z[b + ROT_DIM:b + dim]]
        return jnp.concatenate(parts, axis=0)

    qT_ref[...] = (rope_t(tr[0:attn_w], N_HEADS, HEAD_DIM) * (HEAD_DIM ** -0.5)).astype(qT_ref.dtype)
    iqT_ref[...] = rope_t(tr[attn_w:attn_w + idx_w], IDX_HEADS, IDX_DIM).astype(iqT_ref.dtype)
    vt = tr[attn_w + idx_w:2 * attn_w + idx_w]
    for j in range(n_vt):
        vT_ref[j] = vt[:, j * TK:(j + 1) * TK].astype(vT_ref.dtype)
    wT_ref[...] = tr[2 * attn_w + idx_w:2 * attn_w + idx_w + IDX_HEADS] * (
        (IDX_HEADS ** -0.5) * (IDX_DIM ** -0.5))

    ra = ra_ref[...]
    rb = rb_ref[...]
    rc = rc_ref[...]

    def rope_rows(c):
        return (c * ra + pltpu.roll(c, half, 1) * rb + pltpu.roll(c, LANES - half, 1) * rc)

    for c in range(attn_w // LANES):
        k_ref[:, c * LANES:(c + 1) * LANES] = rope_rows(
            row[:, c * LANES:(c + 1) * LANES]).astype(k_ref.dtype)
    ik_ref[...] = rope_rows(row[:, attn_w:attn_w + LANES])[:, :IDX_DIM].astype(ik_ref.dtype)


def _projection(x, g, wrow, wt, cosT, sinT, ra, rb, rc, tm):
    B, S, D = x.shape
    attn_w = N_HEADS * HEAD_DIM
    idx_w = IDX_HEADS * IDX_DIM
    n_vt = tm // TK
    grid = (B, S // tm)
    const = lambda b, i: (0, 0)
    out_shape = (
        jax.ShapeDtypeStruct((B, attn_w, S), MXU_DTYPE),
        jax.ShapeDtypeStruct((B, idx_w, S), MXU_DTYPE),
        jax.ShapeDtypeStruct((B, S // TK, attn_w, TK), MXU_DTYPE),
        jax.ShapeDtypeStruct((B, IDX_HEADS, S), _F32),
        jax.ShapeDtypeStruct((B, S, attn_w), MXU_DTYPE),
        jax.ShapeDtypeStruct((B, S, IDX_DIM), MXU_DTYPE),
    )
    return pl.pallas_call(
        functools.partial(_proj_kernel, n_vt=n_vt),
        grid=grid,
        in_specs=[
            pl.BlockSpec((None, tm, D), lambda b, i: (b, i, 0)),
            pl.BlockSpec((1, D), const),
            pl.BlockSpec(wrow.shape, const),
            pl.BlockSpec(wt.shape, const),
            pl.BlockSpec((ROT_DIM // 2, tm), lambda b, i: (0, i)),
            pl.BlockSpec((ROT_DIM // 2, tm), lambda b, i: (0, i)),
            pl.BlockSpec((tm, LANES), lambda b, i: (i, 0)),
            pl.BlockSpec((tm, LANES), lambda b, i: (i, 0)),
            pl.BlockSpec((tm, LANES), lambda b, i: (i, 0)),
        ],
        out_specs=(
            pl.BlockSpec((None, attn_w, tm), lambda b, i: (b, 0, i)),
            pl.BlockSpec((None, idx_w, tm), lambda b, i: (b, 0, i)),
            pl.BlockSpec((None, n_vt, attn_w, TK), lambda b, i: (b, i, 0, 0)),
            pl.BlockSpec((None, IDX_HEADS, tm), lambda b, i: (b, 0, i)),
            pl.BlockSpec((None, tm, attn_w), lambda b, i: (b, i, 0)),
            pl.BlockSpec((None, tm, IDX_DIM), lambda b, i: (b, i, 0)),
        ),
        out_shape=out_shape,
        compiler_params=_params(),
        name="proj_rope",
    )(x, g, wrow, wt, cosT, sinT, ra, rb, rc)


def _attn_kernel(iqT_ref, wT_ref, qT_ref, ik_ref, k_ref, vT_ref, o_ref,
                 sc_ref, qpad_ref, acc_ref, m_ref, l_ref, *, ktop, idx_bits):
    i = pl.program_id(1)
    n_rows = (i + 1) * TK
    row_iota = lax.broadcasted_iota(jnp.int32, (TK, TQ), 0)
    col_iota = lax.broadcasted_iota(jnp.int32, (TK, TQ), 1)

    def score_tile(kb, diagonal):
        k0 = pl.multiple_of(kb * TK, TK)
        ikb = ik_ref[pl.ds(k0, TK), :]
        score = jnp.zeros((TK, TQ), _F32)
        for h in range(IDX_HEADS):
            s = jnp.dot(ikb, iqT_ref[h * IDX_DIM:(h + 1) * IDX_DIM, :],
                        preferred_element_type=_F32)
            score = score + jnp.maximum(s, 0.0) * wT_ref[h:h + 1, :]
        score = jnp.where(score == 0.0, 0.0, score)
        bits = pltpu.bitcast(score, jnp.int32)
        key = jnp.where(bits < 0, bits ^ jnp.int32(0x7FFFFFFF), bits)
        if diagonal:
            key = jnp.where(row_iota <= col_iota, key, jnp.int32(INT_MIN))
        sc_ref[pl.ds(k0, TK), :] = key

    def score_body(kb, carry):
        score_tile(kb, False)
        return carry

    lax.fori_loop(0, i, score_body, 0)
    score_tile(i, True)

    sub_iota = lax.broadcasted_iota(jnp.int32, (COUNT_ROWS, TQ), 0)

    def count(pred):
        def body(r, cnt):
            r0 = pl.multiple_of(r * COUNT_ROWS, COUNT_ROWS)
            hit = pred(sc_ref[pl.ds(r0, COUNT_ROWS), :], r0 + sub_iota).astype(jnp.int32)
            for j in range(COUNT_ROWS // SUBLANES):
                cnt = cnt + hit[j * SUBLANES:(j + 1) * SUBLANES]
            return cnt
        cnt = lax.fori_loop(0, n_rows // COUNT_ROWS, body, jnp.zeros((SUBLANES, TQ), jnp.int32))
        return jnp.sum(cnt.astype(_F32), axis=0, keepdims=True)

    def thr_body(it, carry):
        tb, cge = carry
        cand_b = tb | jnp.left_shift(jnp.int32(1), 31 - it)
        cand = cand_b ^ jnp.int32(INT_MIN)
        cnt = count(lambda blk, idx: blk >= cand)
        ok = cnt >= ktop
        return jnp.where(ok, cand_b, tb), jnp.where(ok, cnt, cge)

    zeros_q = jnp.zeros((1, TQ), jnp.int32)
    tb, cge = lax.fori_loop(0, 32, thr_body, (zeros_q, jnp.zeros((1, TQ), _F32)))
    thr = tb ^ jnp.int32(INT_MIN)
    has_thr = tb != 0

    need = ktop - count(lambda blk, idx: blk > thr)
    any_tie = jnp.max(jnp.where(has_thr & (cge > ktop), 1.0, 0.0)) > 0.0

    def tie_search():
        def body(it, p):
            cand = p | jnp.left_shift(jnp.int32(1), idx_bits - 1 - it)
            cnt = count(lambda blk, idx: (blk == thr) & (idx < cand))
            return jnp.where(cnt < need, cand, p)
        return lax.fori_loop(0, idx_bits, body, zeros_q)

    tie_last = lax.cond(any_tie, tie_search,
                        lambda: jnp.full((1, TQ), (1 << idx_bits) - 1, jnp.int32))
    tie_last = jnp.where(has_thr, tie_last, -1)

    pair_rows = lax.broadcasted_iota(jnp.int32, (2 * HEAD_DIM, TQ), 0)
    for h in range(N_HEADS):
        pair = qT_ref[(h // 2) * 2 * HEAD_DIM:(h // 2 + 1) * 2 * HEAD_DIM, :].astype(_F32)
        mine = (pair_rows >= HEAD_DIM) if h % 2 else (pair_rows < HEAD_DIM)
        qpad_ref[h] = jnp.where(mine, pair, 0.0).astype(qpad_ref.dtype)
    m_ref[...] = jnp.full(m_ref.shape, NEG_BIG, _F32)
    l_ref[...] = jnp.zeros(l_ref.shape, _F32)
    acc_ref[...] = jnp.zeros(acc_ref.shape, _F32)

    def flash_body(kb, carry):
        k0 = pl.multiple_of(kb * TK, TK)
        key = sc_ref[pl.ds(k0, TK), :]
        keep = (key > thr) | ((key == thr) & ((k0 + row_iota) <= tie_last))
        for h in range(N_HEADS):
            kpair = k_ref[pl.ds(k0, TK), (h // 2) * 2 * HEAD_DIM:(h // 2 + 1) * 2 * HEAD_DIM]
            s = jnp.dot(kpair, qpad_ref[h], preferred_element_type=_F32)
            s = jnp.where(keep, s, NEG_BIG)
            m_old = m_ref[h:h + 1, :]
            m_new = jnp.maximum(m_old, jnp.max(s, axis=0, keepdims=True))
            alpha = jnp.exp(m_old - m_new)
            p = jnp.exp(s - m_new)
            l_ref[h:h + 1, :] = alpha * l_ref[h:h + 1, :] + jnp.sum(p, axis=0, keepdims=True)
            m_ref[h:h + 1, :] = m_new
            pv = jnp.dot(vT_ref[kb, h * HEAD_DIM:(h + 1) * HEAD_DIM, :], p.astype(MXU_DTYPE),
                         preferred_element_type=_F32)
            acc_ref[h * HEAD_DIM:(h + 1) * HEAD_DIM, :] = (
                alpha * acc_ref[h * HEAD_DIM:(h + 1) * HEAD_DIM, :] + pv)
        return carry

    lax.fori_loop(0, i + 1, flash_body, 0)

    for h in range(N_HEADS):
        acc_ref[h * HEAD_DIM:(h + 1) * HEAD_DIM, :] = (
            acc_ref[h * HEAD_DIM:(h + 1) * HEAD_DIM, :] / l_ref[h:h + 1, :])
    o_ref[...] = acc_ref[...].T.astype(o_ref.dtype)


def _attention(qT, iqT, vT, wT, k, ik, ktop):
    B, attn_w, S = qT.shape
    idx_w = iqT.shape[1]
    idx_bits = max(1, math.ceil(math.log2(S)))
    grid = (B, S // TQ)
    return pl.pallas_call(
        functools.partial(_attn_kernel, ktop=ktop, idx_bits=idx_bits),
        grid=grid,
        in_specs=[
            pl.BlockSpec((None, idx_w, TQ), lambda b, i: (b, 0, i)),
            pl.BlockSpec((None, IDX_HEADS, TQ), lambda b, i: (b, 0, i)),
            pl.BlockSpec((None, attn_w, TQ), lambda b, i: (b, 0, i)),
            _resident((None, S, IDX_DIM), lambda b, i: (b, 0, 0)),
            _resident((None, S, attn_w), lambda b, i: (b, 0, 0)),
            _resident((None, S // TK, attn_w, TK), lambda b, i: (b, 0, 0, 0)),
        ],
        out_specs=pl.BlockSpec((None, TQ, attn_w), lambda b, i: (b, i, 0)),
        out_shape=jax.ShapeDtypeStruct((B, S, attn_w), MXU_DTYPE),
        scratch_shapes=[
            pltpu.VMEM((S, TQ), jnp.int32),
            pltpu.VMEM((N_HEADS, 2 * HEAD_DIM, TQ), MXU_DTYPE),
            pltpu.VMEM((attn_w, TQ), _F32),
            pltpu.VMEM((N_HEADS, TQ), _F32),
            pltpu.VMEM((N_HEADS, TQ), _F32),
        ],
        compiler_params=_params(),
        name="topk_attention",
    )(iqT, wT, qT, ik, k, vT)


def _mixer_kernel(x_ref, halo_ref, o_ref, g_ref, wpool_ref, wgate_ref, poolw_ref, pscale_ref,
                  wpp_ref, wap_ref, wout_ref, h_ref, u_ref, *, tm):
    i = pl.program_id(1)
    D = x_ref.shape[-1]
    group_dim = wpool_ref.shape[1] // POOL_GROUPS
    g = g_ref[...]
    x = x_ref[...]
    xn = _rmsnorm(x, g).astype(MXU_DTYPE)
    hn = _rmsnorm(halo_ref[...], g).astype(MXU_DTYPE)
    u_halo = jnp.dot(hn, wpool_ref[...], preferred_element_type=_F32)
    u_ref[0:POOL_HALO, :] = jnp.where(i > 0, u_halo, 0.0)
    u_ref[POOL_HALO:POOL_HALO + tm, :] = jnp.dot(xn, wpool_ref[...], preferred_element_type=_F32)

    t = i * tm + lax.broadcasted_iota(jnp.int32, (tm, group_dim), 0)
    mixed = []
    for gi, w in enumerate(POOL_WINDOWS):
        lanes = slice(gi * group_dim, (gi + 1) * group_dim)
        cur = u_ref[POOL_HALO:POOL_HALO + tm, lanes]
        sums = cur
        for j in range(1, w):
            sums = sums + u_ref[POOL_HALO - j:POOL_HALO - j + tm, lanes]
        cnt = jnp.minimum(t + 1, w).astype(_F32)
        pooled = sums / cnt - cur
        mixed.append(jnp.dot(pooled.astype(MXU_DTYPE), poolw_ref[gi], preferred_element_type=_F32))
    mixed = jnp.concatenate(mixed, axis=-1) * pscale_ref[...]
    y_pool = jnp.dot(mixed.astype(MXU_DTYPE), wpp_ref[...], preferred_element_type=_F32)
    y_attn = jnp.dot(o_ref[...], wap_ref[...], preferred_element_type=_F32)
    gates = jax.nn.sigmoid(jnp.dot(xn, wgate_ref[...], preferred_element_type=_F32))
    merged = gates[:, :D] * y_pool + gates[:, D:] * y_attn
    h_ref[...] = x + jnp.dot(merged.astype(MXU_DTYPE), wout_ref[...], preferred_element_type=_F32)


def _mixer_out(x, o, g, wpool, wgate, poolw, pscale, wpp, wap, wout, tm):
    B, S, D = x.shape
    pool_w = wpool.shape[1]
    attn_w = o.shape[-1]
    grid = (B, S // tm)
    const2 = lambda b, i: (0, 0)
    halo_blocks = tm // POOL_HALO
    return pl.pallas_call(
        functools.partial(_mixer_kernel, tm=tm),
        grid=grid,
        in_specs=[
            pl.BlockSpec((None, tm, D), lambda b, i: (b, i, 0)),
            pl.BlockSpec((None, POOL_HALO, D),
                         lambda b, i: (b, jnp.maximum(i * halo_blocks - 1, 0), 0)),
            pl.BlockSpec((None, tm, attn_w), lambda b, i: (b, i, 0)),
            pl.BlockSpec((1, D), const2),
            pl.BlockSpec(wpool.shape, const2),
            pl.BlockSpec(wgate.shape, const2),
            pl.BlockSpec(poolw.shape, lambda b, i: (0, 0, 0)),
            pl.BlockSpec((1, pool_w), const2),
            pl.BlockSpec(wpp.shape, const2),
            pl.BlockSpec(wap.shape, const2),
            pl.BlockSpec(wout.shape, const2),
        ],
        out_specs=pl.BlockSpec((None, tm, D), lambda b, i: (b, i, 0)),
        out_shape=jax.ShapeDtypeStruct((B, S, D), _F32),
        scratch_shapes=[pltpu.VMEM((POOL_HALO + tm, pool_w), _F32)],
        compiler_params=_params(),
        name="mixer_out",
    )(x, x, o, g, wpool, wgate, poolw, pscale, wpp, wap, wout)


def _ffn_kernel(h_ref, halo_ref, g_ref, wa_ref, wb_ref, cwa_ref, cwb_ref, cba_ref, cbb_ref,
                wd_ref, gf_ref, out_ref, hn_ref, ua_ref, ub_ref, acc_ref, *, tm, n_chunks):
    i = pl.program_id(1)
    g = g_ref[...]
    h = h_ref[...]
    hn_ref[0:FFN_HALO, :] = _rmsnorm(halo_ref[...], g).astype(MXU_DTYPE)
    hn_ref[FFN_HALO:FFN_HALO + tm, :] = _rmsnorm(h, g).astype(MXU_DTYPE)
    acc_ref[...] = jnp.zeros(acc_ref.shape, _F32)
    first = i == 0

    def conv(u_ref, up, cw, cb):
        u_ref[FFN_HALO:FFN_HALO + tm, :] = up[FFN_HALO:, :]
        u_ref[0:FFN_HALO, :] = jnp.where(first, 0.0, up[0:FFN_HALO, :])
        out = cb
        for j in range(CONV_WIDTH):
            off = FFN_HALO - (CONV_WIDTH - 1) + j
            out = out + u_ref[off:off + tm, :] * cw[j:j + 1, :]
        return out

    def chunk(c, carry):
        hn = hn_ref[...]
        a = conv(ua_ref, jnp.dot(hn, wa_ref[c], preferred_element_type=_F32), cwa_ref[c], cba_ref[c])
        b = conv(ub_ref, jnp.dot(hn, wb_ref[c], preferred_element_type=_F32), cwb_ref[c], cbb_ref[c])
        act = (a * jax.nn.sigmoid(a) * b).astype(MXU_DTYPE)
        acc_ref[...] += jnp.dot(act, wd_ref[c], preferred_element_type=_F32)
        return carry

    lax.fori_loop(0, n_chunks, chunk, 0)
    out_ref[...] = _rmsnorm(h + acc_ref[...], gf_ref[...])


def _conv_ffn(h, g, wa, wb, cwa, cwb, cba, cbb, wd, gf, tm):
    B, S, D = h.shape
    n_chunks = wa.shape[0]
    grid = (B, S // tm)
    const2 = lambda b, i: (0, 0)
    const3 = lambda b, i: (0, 0, 0)
    halo_blocks = tm // FFN_HALO
    return pl.pallas_call(
        functools.partial(_ffn_kernel, tm=tm, n_chunks=n_chunks),
        grid=grid,
        in_specs=[
            pl.BlockSpec((None, tm, D), lambda b, i: (b, i, 0)),
            pl.BlockSpec((None, FFN_HALO, D),
                         lambda b, i: (b, jnp.maximum(i * halo_blocks - 1, 0), 0)),
            pl.BlockSpec((1, D), const2),
            pl.BlockSpec(wa.shape, const3),
            pl.BlockSpec(wb.shape, const3),
            pl.BlockSpec(cwa.shape, const3),
            pl.BlockSpec(cwb.shape, const3),
            pl.BlockSpec(cba.shape, const3),
            pl.BlockSpec(cbb.shape, const3),
            pl.BlockSpec(wd.shape, const3),
            pl.BlockSpec((1, D), const2),
        ],
        out_specs=pl.BlockSpec((None, tm, D), lambda b, i: (b, i, 0)),
        out_shape=jax.ShapeDtypeStruct((B, S, D), _F32),
        scratch_shapes=[
            pltpu.VMEM((FFN_HALO + tm, D), MXU_DTYPE),
            pltpu.VMEM((FFN_HALO + tm, FF_CHUNK), _F32),
            pltpu.VMEM((FFN_HALO + tm, FF_CHUNK), _F32),
            pltpu.VMEM((tm, D), _F32),
        ],
        compiler_params=_params(),
        name="conv_ffn",
    )(h, h, g, wa, wb, cwa, cwb, cba, cbb, wd, gf)


def _rope_tables(S):
    half = ROT_DIM // 2
    inv_freq = 1.0 / (ROPE_THETA ** (jnp.arange(half, dtype=_F32) * 2.0 / ROT_DIM))
    ang = jnp.arange(S, dtype=_F32)[:, None] * inv_freq[None, :]
    cos, sin = jnp.cos(ang), jnp.sin(ang)
    zeros = jnp.zeros((S, HEAD_DIM - ROT_DIM), _F32)
    zh = jnp.zeros((S, half), _F32)
    ra = jnp.concatenate([cos, cos, jnp.ones_like(zeros)], axis=1)
    rb = jnp.concatenate([zh, sin, zeros], axis=1)
    rc = jnp.concatenate([-sin, zh, zeros], axis=1)
    rep = LANES // HEAD_DIM
    tile = lambda a: jnp.tile(a, (1, rep))
    return cos.T, sin.T, tile(ra), tile(rb), tile(rc)


def _layer(h, norm_mix_g, w_in, pool_w, pool_scale, w_pool_proj, w_attn_proj, w_out,
           norm_ffn_g, w_up, conv_w, conv_b, w_down, norm_out_g, tables):
    B, S, D = h.shape
    pool_width = D // 2
    attn_w = N_HEADS * HEAD_DIM
    idx_w = IDX_HEADS * IDX_DIM
    d_ff = w_down.shape[0]
    ktop = min(TOPK_MAX, S // 4)
    c0 = pool_width
    cq, ck, cv = c0, c0 + attn_w, c0 + 2 * attn_w
    ciq = c0 + 3 * attn_w
    cik = ciq + idx_w
    ciw = cik + IDX_DIM
    cg = ciw + IDX_HEADS
    cast = lambda a: a.astype(MXU_DTYPE)

    wrow = cast(jnp.concatenate(
        [w_in[:, ck:cv], w_in[:, cik:ciw], jnp.zeros((D, LANES - IDX_DIM), w_in.dtype)], axis=1))
    wt = cast(jnp.concatenate(
        [w_in[:, cq:ck], w_in[:, ciq:cik], w_in[:, cv:ciq], w_in[:, ciw:cg],
         jnp.zeros((D, 16 - IDX_HEADS), w_in.dtype)], axis=1).T)
    row1 = lambda a: a.reshape(1, -1)

    tm = 512 if S % 512 == 0 else TQ
    qT, iqT, vT, wT, k, ik = _projection(h, row1(norm_mix_g), wrow, wt, *tables, tm=tm)
    o = _attention(qT, iqT, vT, wT, k, ik, ktop)
    h1 = _mixer_out(h, o, row1(norm_mix_g), cast(w_in[:, :c0]), cast(w_in[:, cg:]),
                    cast(pool_w), row1(pool_scale), cast(w_pool_proj), cast(w_attn_proj),
                    cast(w_out), tm=TQ)

    n_chunks = d_ff // FF_CHUNK
    split_cols = lambda a: a.reshape(a.shape[0], n_chunks, FF_CHUNK).transpose(1, 0, 2)
    wa, wb = cast(split_cols(w_up[:, :d_ff])), cast(split_cols(w_up[:, d_ff:]))
    cwa, cwb = split_cols(conv_w[:, :d_ff]), split_cols(conv_w[:, d_ff:])
    cba, cbb = split_cols(conv_b[None, :d_ff]), split_cols(conv_b[None, d_ff:])
    wd = cast(w_down.reshape(n_chunks, FF_CHUNK, D))
    return _conv_ffn(h1, row1(norm_ffn_g), wa, wb, cwa, cwb, cba, cbb, wd, row1(norm_out_g), tm=TQ)


def kernel(x, norm_mix_g, w_in, pool_w, pool_scale, w_pool_proj, w_attn_proj, w_out, norm_ffn_g,
           w_up, conv_w, conv_b, w_down, norm_final_g):
    depth = w_in.shape[0]
    assert depth == 1, "the final RMSNorm is fused into the last layer's ConvFFN kernel"
    S = x.shape[1]
    assert S % TQ == 0 and TQ == TK
    tables = _rope_tables(S)
    return _layer(x, norm_mix_g[0], w_in[0], pool_w[0], pool_scale[0], w_pool_proj[0],
                  w_attn_proj[0], w_out[0], norm_ffn_g[0], w_up[0], conv_w[0], conv_b[0],
                  w_down[0], norm_final_g, tables)
```

```python
import functools
import math

import jax
import jax.numpy as jnp
from jax import lax
from jax.experimental import pallas as pl
from jax.experimental.pallas import tpu as pltpu

POOL_GROUPS = 4
POOL_WINDOWS = (2, 4, 8, 16)
N_HEADS = 8
HEAD_DIM = 64
ROT_DIM = HEAD_DIM // 4
ROPE_THETA = 500000.0
IDX_HEADS = 8
IDX_DIM = 64
TOPK_MAX = 256
N_BRANCH = 2
CONV_WIDTH = 3
EPS = 1e-6

LANES = 128
SUBLANES = 8
VMEM_LIMIT_BYTES = 56 * 1024 * 1024
MXU_DTYPE = jnp.bfloat16

TQ = 256
TK = 256
TKF = 512
SCORE_ROWS = 128
COUNT_ROWS = 256
V_ROWS = HEAD_DIM + 16
LOG2E = 1.4426950408889634
POOL_HALO = 16
FFN_HALO = 16
FF_CHUNK = 256

INT_MIN = -(2 ** 31)
NEG_BIG = -1e30

_F32 = jnp.float32


def _rmsnorm(x, g):
    return x * lax.rsqrt(jnp.mean(x * x, axis=-1, keepdims=True) + EPS) * g


def _params(**kw):
    return pltpu.CompilerParams(vmem_limit_bytes=VMEM_LIMIT_BYTES, **kw)


def _resident(block_shape, index_map):
    return pl.BlockSpec(block_shape, index_map, pipeline_mode=pl.Buffered(1))


def _proj_kernel(x_ref, g_ref, wrow_ref, wt_ref, cos_ref, sin_ref, ra_ref, rb_ref, rc_ref,
                 qT_ref, iqT_ref, vT_ref, wT_ref, k_ref, ik_ref, *, n_vt):
    attn_w = N_HEADS * HEAD_DIM
    idx_w = IDX_HEADS * IDX_DIM
    xn = _rmsnorm(x_ref[...], g_ref[...]).astype(MXU_DTYPE)
    row = jnp.dot(xn, wrow_ref[...], preferred_element_type=_F32)
    tr = lax.dot_general(wt_ref[...], xn, (((1,), (1,)), ((), ())),
                         preferred_element_type=_F32)

    cos = cos_ref[...]
    sin = sin_ref[...]
    half = ROT_DIM // 2

    def rope_t(z, n_heads, dim):
        parts = []
        for h in range(n_heads):
            b = h * dim
            x1 = z[b:b + half]
            x2 = z[b + half:b + ROT_DIM]
            parts += [x1 * cos - x2 * sin, x2 * cos + x1 * sin, z[b + ROT_DIM:b + dim]]
        return jnp.concatenate(parts, axis=0)

    qT_ref[...] = (rope_t(tr[0:attn_w], N_HEADS, HEAD_DIM)
                   * (HEAD_DIM ** -0.5 * LOG2E)).astype(qT_ref.dtype)
    iqT_ref[...] = rope_t(tr[attn_w:attn_w + idx_w], IDX_HEADS, IDX_DIM).astype(iqT_ref.dtype)
    vt = tr[attn_w + idx_w:2 * attn_w + idx_w]
    ones_rows = (lax.broadcasted_iota(jnp.int32, (V_ROWS - HEAD_DIM, TKF), 0) == 0).astype(_F32)
    for j in range(n_vt):
        for h in range(N_HEADS):
            vT_ref[j, h, 0:HEAD_DIM, :] = vt[h * HEAD_DIM:(h + 1) * HEAD_DIM,
                                             j * TKF:(j + 1) * TKF].astype(vT_ref.dtype)
            vT_ref[j, h, HEAD_DIM:V_ROWS, :] = ones_rows.astype(vT_ref.dtype)
    wT_ref[...] = tr[2 * attn_w + idx_w:2 * attn_w + idx_w + IDX_HEADS] * (
        (IDX_HEADS ** -0.5) * (IDX_DIM ** -0.5))

    ra = ra_ref[...]
    rb = rb_ref[...]
    rc = rc_ref[...]

    def rope_rows(c):
        return (c * ra + pltpu.roll(c, half, 1) * rb + pltpu.roll(c, LANES - half, 1) * rc)

    for c in range(attn_w // LANES):
        k_ref[:, c * LANES:(c + 1) * LANES] = rope_rows(
            row[:, c * LANES:(c + 1) * LANES]).astype(k_ref.dtype)
    ik_ref[...] = rope_rows(row[:, attn_w:attn_w + LANES])[:, :IDX_DIM].astype(ik_ref.dtype)


def _projection(x, g, wrow, wt, cosT, sinT, ra, rb, rc, tm):
    B, S, D = x.shape
    attn_w = N_HEADS * HEAD_DIM
    idx_w = IDX_HEADS * IDX_DIM
    n_vt = tm // TKF
    grid = (B, S // tm)
    const = lambda b, i: (0, 0)
    out_shape = (
        jax.ShapeDtypeStruct((B, attn_w, S), MXU_DTYPE),
        jax.ShapeDtypeStruct((B, idx_w, S), MXU_DTYPE),
        jax.ShapeDtypeStruct((B, S // TKF, N_HEADS, V_ROWS, TKF), MXU_DTYPE),
        jax.ShapeDtypeStruct((B, IDX_HEADS, S), _F32),
        jax.ShapeDtypeStruct((B, S, attn_w), MXU_DTYPE),
        jax.ShapeDtypeStruct((B, S, IDX_DIM), MXU_DTYPE),
    )
    return pl.pallas_call(
        functools.partial(_proj_kernel, n_vt=n_vt),
        grid=grid,
        in_specs=[
            pl.BlockSpec((None, tm, D), lambda b, i: (b, i, 0)),
            pl.BlockSpec((1, D), const),
            pl.BlockSpec(wrow.shape, const),
            pl.BlockSpec(wt.shape, const),
            pl.BlockSpec((ROT_DIM // 2, tm), lambda b, i: (0, i)),
            pl.BlockSpec((ROT_DIM // 2, tm), lambda b, i: (0, i)),
            pl.BlockSpec((tm, LANES), lambda b, i: (i, 0)),
            pl.BlockSpec((tm, LANES), lambda b, i: (i, 0)),
            pl.BlockSpec((tm, LANES), lambda b, i: (i, 0)),
        ],
        out_specs=(
            pl.BlockSpec((None, attn_w, tm), lambda b, i: (b, 0, i)),
            pl.BlockSpec((None, idx_w, tm), lambda b, i: (b, 0, i)),
            pl.BlockSpec((None, n_vt, N_HEADS, V_ROWS, TKF), lambda b, i: (b, i, 0, 0, 0)),
            pl.BlockSpec((None, IDX_HEADS, tm), lambda b, i: (b, 0, i)),
            pl.BlockSpec((None, tm, attn_w), lambda b, i: (b, i, 0)),
            pl.BlockSpec((None, tm, IDX_DIM), lambda b, i: (b, i, 0)),
        ),
        out_shape=out_shape,
        compiler_params=_params(),
        name="proj_rope",
    )(x, g, wrow, wt, cosT, sinT, ra, rb, rc)


def _attn_kernel(iqT_ref, wT_ref, qT_ref, ik_ref, k_ref, vT_ref, o_ref,
                 sc_ref, qpad_ref, acc_ref, m_ref, alpha_ref, bias_ref, on_ref, *stage_refs,
                 ktop, idx_bits):
    s_refs, p_refs = stage_refs[:N_HEADS], stage_refs[N_HEADS:]
    i = pl.program_id(1)
    stage = 0
    n_rows = (i + 1) * TK

    sub_rows = lax.broadcasted_iota(jnp.int32, (SCORE_ROWS, TQ), 0)
    sub_cols = lax.broadcasted_iota(jnp.int32, (SCORE_ROWS, TQ), 1)

    def score_tile(kb, diagonal):
        k0 = pl.multiple_of(kb * TK, TK)
        for sub in range(TK // SCORE_ROWS):
            r0 = k0 + sub * SCORE_ROWS
            ikb = ik_ref[pl.ds(r0, SCORE_ROWS), :]
            score = None
            for h in range(IDX_HEADS):
                s = jnp.dot(ikb, iqT_ref[h * IDX_DIM:(h + 1) * IDX_DIM, :],
                            preferred_element_type=_F32)
                term = jnp.maximum(s, 0.0) * wT_ref[h:h + 1, :]
                score = term if score is None else score + term
            score = jnp.where(score == 0.0, 0.0, score)
            bits = pltpu.bitcast(score, jnp.int32)
            key = jnp.where(bits < 0, bits ^ jnp.int32(0x7FFFFFFF), bits)
            if diagonal:
                key = jnp.where(sub_rows + sub * SCORE_ROWS <= sub_cols, key, jnp.int32(INT_MIN))
            sc_ref[pl.ds(r0, SCORE_ROWS), :] = key

    def score_body(kb, carry):
        score_tile(kb, False)
        return carry

    lax.fori_loop(0, i, score_body, 0)
    score_tile(i, True)

    row8_iota = lax.broadcasted_iota(jnp.int32, (SUBLANES, TQ), 0)

    def count(pred):
        def body(r, cnt):
            r0 = pl.multiple_of(r * COUNT_ROWS, COUNT_ROWS)
            blk = sc_ref[pl.ds(r0, COUNT_ROWS), :]
            parts = []
            for j in range(COUNT_ROWS // SUBLANES):
                parts.append(pred(blk[j * SUBLANES:(j + 1) * SUBLANES],
                                  r0 + j * SUBLANES + row8_iota).astype(jnp.int32))
            while len(parts) > 1:
                parts = [parts[a] + parts[a + 1] for a in range(0, len(parts), 2)]
            return cnt + parts[0]
        cnt = lax.fori_loop(0, n_rows // COUNT_ROWS, body, jnp.zeros((SUBLANES, TQ), jnp.int32))
        return jnp.sum(cnt.astype(_F32), axis=0, keepdims=True)

    def thr_body(it, carry):
        tb, cge = carry
        cand_b = tb | jnp.left_shift(jnp.int32(1), 31 - it)
        cand = cand_b ^ jnp.int32(INT_MIN)
        cnt = count(lambda blk, idx: blk >= cand)
        ok = cnt >= ktop
        return jnp.where(ok, cand_b, tb), jnp.where(ok, cnt, cge)

    zeros_q = jnp.zeros((1, TQ), jnp.int32)
    tb, cge = lax.fori_loop(0, 32, thr_body, (zeros_q, jnp.zeros((1, TQ), _F32)))
    thr = tb ^ jnp.int32(INT_MIN)
    has_thr = tb != 0

    need = ktop - count(lambda blk, idx: blk > thr)
    any_tie = jnp.max(jnp.where(has_thr & (cge > ktop), 1.0, 0.0)) > 0.0

    @pl.when(any_tie)
    def _demote_surplus_ties():
        def body(it, p):
            cand = p | jnp.left_shift(jnp.int32(1), idx_bits - 1 - it)
            cnt = count(lambda blk, idx: (blk == thr) & (idx < cand))
            return jnp.where(cnt < need, cand, p)
        tie_last = lax.fori_loop(0, idx_bits, body, zeros_q)
        blk_iota = lax.broadcasted_iota(jnp.int32, (COUNT_ROWS, TQ), 0)

        def demote(r, carry):
            r0 = pl.multiple_of(r * COUNT_ROWS, COUNT_ROWS)
            blk = sc_ref[pl.ds(r0, COUNT_ROWS), :]
            surplus = has_thr & (blk == thr) & ((r0 + blk_iota) > tie_last)
            sc_ref[pl.ds(r0, COUNT_ROWS), :] = jnp.where(surplus, thr - 1, blk)
            return carry
        lax.fori_loop(0, n_rows // COUNT_ROWS, demote, 0)

    thr_ge = jnp.where(has_thr, thr, jnp.int32(INT_MIN + 1))

    pair_rows = lax.broadcasted_iota(jnp.int32, (2 * HEAD_DIM, TQ), 0)
    for h in range(N_HEADS):
        pair = qT_ref[(h // 2) * 2 * HEAD_DIM:(h // 2 + 1) * 2 * HEAD_DIM, :].astype(_F32)
        mine = (pair_rows >= HEAD_DIM) if h % 2 else (pair_rows < HEAD_DIM)
        qpad_ref[h] = jnp.where(mine, pair, 0.0).astype(qpad_ref.dtype)
    m_ref[...] = jnp.full(m_ref.shape, NEG_BIG, _F32)
    acc_ref[...] = jnp.zeros(acc_ref.shape, _F32)

    @pl.when((i + 1) % (TKF // TK) != 0)
    def _pad_keys():
        sc_ref[pl.ds(pl.multiple_of(n_rows, TK), TK), :] = jnp.full((TK, TQ), INT_MIN, jnp.int32)

    def flash_body(kb, carry):
        k0 = pl.multiple_of(kb * TKF, TKF)
        bias_ref[...] = jnp.where(sc_ref[pl.ds(k0, TKF), :] >= thr_ge, 0.0, NEG_BIG)
        for h in range(N_HEADS):
            kpair = k_ref[pl.ds(k0, TKF), (h // 2) * 2 * HEAD_DIM:(h // 2 + 1) * 2 * HEAD_DIM]
            s_refs[h][0] = jnp.dot(kpair, qpad_ref[h], preferred_element_type=_F32)
        for h in range(N_HEADS):
            col_max = [None] * 4
            for j in range(TKF // SUBLANES):
                rows = slice(j * SUBLANES, (j + 1) * SUBLANES)
                slab = s_refs[h][stage, rows, :] + bias_ref[rows, :]
                c = j % len(col_max)
                col_max[c] = slab if col_max[c] is None else jnp.maximum(col_max[c], slab)
            tile_max = jnp.maximum(jnp.maximum(col_max[0], col_max[1]),
                                   jnp.maximum(col_max[2], col_max[3]))
            m_old = m_ref[h:h + 1, :]
            m_new = jnp.maximum(m_old, jnp.max(tile_max, axis=0, keepdims=True))
            alpha_ref[h:h + 1, :] = jnp.exp2(m_old - m_new)
            m_ref[h:h + 1, :] = m_new
        for h in range(N_HEADS):
            p_refs[h][0] = jnp.exp2(s_refs[h][stage] + bias_ref[...]
                                    - m_ref[h:h + 1, :]).astype(p_refs[h].dtype)
        for h in range(N_HEADS):
            pv = jnp.dot(vT_ref[kb, h], p_refs[h][stage], preferred_element_type=_F32)
            acc_ref[h] = alpha_ref[h:h + 1, :] * acc_ref[h] + pv
        return carry

    lax.fori_loop(0, (i + TKF // TK) // (TKF // TK), flash_body, 0)

    for h in range(N_HEADS):
        on_ref[h * HEAD_DIM:(h + 1) * HEAD_DIM, :] = (
            acc_ref[h, 0:HEAD_DIM, :] / acc_ref[h, HEAD_DIM:HEAD_DIM + 1, :])
    o_ref[...] = on_ref[...].T.astype(o_ref.dtype)


def _attention(qT, iqT, vT, wT, k, ik, ktop):
    B, attn_w, S = qT.shape
    idx_w = iqT.shape[1]
    idx_bits = max(1, math.ceil(math.log2(S)))
    grid = (B, S // TQ)
    return pl.pallas_call(
        functools.partial(_attn_kernel, ktop=ktop, idx_bits=idx_bits),
        grid=grid,
        in_specs=[
            pl.BlockSpec((None, idx_w, TQ), lambda b, i: (b, 0, i)),
            pl.BlockSpec((None, IDX_HEADS, TQ), lambda b, i: (b, 0, i)),
            pl.BlockSpec((None, attn_w, TQ), lambda b, i: (b, 0, i)),
            _resident((None, S, IDX_DIM), lambda b, i: (b, 0, 0)),
            _resident((None, S, attn_w), lambda b, i: (b, 0, 0)),
            _resident((None, S // TKF, N_HEADS, V_ROWS, TKF), lambda b, i: (b, 0, 0, 0, 0)),
        ],
        out_specs=pl.BlockSpec((None, TQ, attn_w), lambda b, i: (b, i, 0)),
        out_shape=jax.ShapeDtypeStruct((B, S, attn_w), MXU_DTYPE),
        scratch_shapes=[
            pltpu.VMEM((S, TQ), jnp.int32),
            pltpu.VMEM((N_HEADS, 2 * HEAD_DIM, TQ), MXU_DTYPE),
            pltpu.VMEM((N_HEADS, V_ROWS, TQ), _F32),
            pltpu.VMEM((N_HEADS, TQ), _F32),
            pltpu.VMEM((N_HEADS, TQ), _F32),
            pltpu.VMEM((TKF, TQ), _F32),
            pltpu.VMEM((attn_w, TQ), _F32),
        ] + [pltpu.VMEM((1, TKF, TQ), _F32)] * N_HEADS
          + [pltpu.VMEM((1, TKF, TQ), MXU_DTYPE)] * N_HEADS,
        compiler_params=_params(),
        name="topk_attention",
    )(iqT, wT, qT, ik, k, vT)


def _mixer_kernel(x_ref, halo_ref, o_ref, g_ref, wpool_ref, wgate_ref, poolw_ref, pscale_ref,
                  wpp_ref, wap_ref, wout_ref, h_ref, u_ref, *, tm):
    i = pl.program_id(1)
    D = x_ref.shape[-1]
    group_dim = wpool_ref.shape[1] // POOL_GROUPS
    g = g_ref[...]
    x = x_ref[...]
    xn = _rmsnorm(x, g).astype(MXU_DTYPE)
    hn = _rmsnorm(halo_ref[...], g).astype(MXU_DTYPE)
    u_halo = jnp.dot(hn, wpool_ref[...], preferred_element_type=_F32)
    u_ref[0:POOL_HALO, :] = jnp.where(i > 0, u_halo, 0.0)
    u_ref[POOL_HALO:POOL_HALO + tm, :] = jnp.dot(xn, wpool_ref[...], preferred_element_type=_F32)

    t = i * tm + lax.broadcasted_iota(jnp.int32, (tm, group_dim), 0)
    mixed = []
    for gi, w in enumerate(POOL_WINDOWS):
        lanes = slice(gi * group_dim, (gi + 1) * group_dim)
        cur = u_ref[POOL_HALO:POOL_HALO + tm, lanes]
        sums = cur
        for j in range(1, w):
            sums = sums + u_ref[POOL_HALO - j:POOL_HALO - j + tm, lanes]
        cnt = jnp.minimum(t + 1, w).astype(_F32)
        pooled = sums / cnt - cur
        mixed.append(jnp.dot(pooled.astype(MXU_DTYPE), poolw_ref[gi], preferred_element_type=_F32))
    mixed = jnp.concatenate(mixed, axis=-1) * pscale_ref[...]
    y_pool = jnp.dot(mixed.astype(MXU_DTYPE), wpp_ref[...], preferred_element_type=_F32)
    y_attn = jnp.dot(o_ref[...], wap_ref[...], preferred_element_type=_F32)
    gates = jax.nn.sigmoid(jnp.dot(xn, wgate_ref[...], preferred_element_type=_F32))
    merged = gates[:, :D] * y_pool + gates[:, D:] * y_attn
    h_ref[...] = x + jnp.dot(merged.astype(MXU_DTYPE), wout_ref[...], preferred_element_type=_F32)


def _mixer_out(x, o, g, wpool, wgate, poolw, pscale, wpp, wap, wout, tm):
    B, S, D = x.shape
    pool_w = wpool.shape[1]
    attn_w = o.shape[-1]
    grid = (B, S // tm)
    const2 = lambda b, i: (0, 0)
    halo_blocks = tm // POOL_HALO
    return pl.pallas_call(
        functools.partial(_mixer_kernel, tm=tm),
        grid=grid,
        in_specs=[
            pl.BlockSpec((None, tm, D), lambda b, i: (b, i, 0)),
            pl.BlockSpec((None, POOL_HALO, D),
                         lambda b, i: (b, jnp.maximum(i * halo_blocks - 1, 0), 0)),
            pl.BlockSpec((None, tm, attn_w), lambda b, i: (b, i, 0)),
            pl.BlockSpec((1, D), const2),
            pl.BlockSpec(wpool.shape, const2),
            pl.BlockSpec(wgate.shape, const2),
            pl.BlockSpec(poolw.shape, lambda b, i: (0, 0, 0)),
            pl.BlockSpec((1, pool_w), const2),
            pl.BlockSpec(wpp.shape, const2),
            pl.BlockSpec(wap.shape, const2),
            pl.BlockSpec(wout.shape, const2),
        ],
        out_specs=pl.BlockSpec((None, tm, D), lambda b, i: (b, i, 0)),
        out_shape=jax.ShapeDtypeStruct((B, S, D), _F32),
        scratch_shapes=[pltpu.VMEM((POOL_HALO + tm, pool_w), _F32)],
        compiler_params=_params(),
        name="mixer_out",
    )(x, x, o, g, wpool, wgate, poolw, pscale, wpp, wap, wout)


def _ffn_kernel(h_ref, halo_ref, g_ref, wa_ref, wb_ref, cwa_ref, cwb_ref, cba_ref, cbb_ref,
                wd_ref, gf_ref, out_ref, hn_ref, ua_ref, ub_ref, acc_ref, *, tm, n_chunks):
    i = pl.program_id(1)
    g = g_ref[...]
    h = h_ref[...]
    hn_ref[0:FFN_HALO, :] = _rmsnorm(halo_ref[...], g).astype(MXU_DTYPE)
    hn_ref[FFN_HALO:FFN_HALO + tm, :] = _rmsnorm(h, g).astype(MXU_DTYPE)
    acc_ref[...] = jnp.zeros(acc_ref.shape, _F32)
    first = i == 0

    def conv(u_ref, up, cw, cb):
        u_ref[FFN_HALO:FFN_HALO + tm, :] = up[FFN_HALO:, :]
        u_ref[0:FFN_HALO, :] = jnp.where(first, 0.0, up[0:FFN_HALO, :])
        out = cb
        for j in range(CONV_WIDTH):
            off = FFN_HALO - (CONV_WIDTH - 1) + j
            out = out + u_ref[off:off + tm, :] * cw[j:j + 1, :]
        return out

    def chunk(c, carry):
        hn = hn_ref[...]
        a = conv(ua_ref, jnp.dot(hn, wa_ref[c], preferred_element_type=_F32), cwa_ref[c], cba_ref[c])
        b = conv(ub_ref, jnp.dot(hn, wb_ref[c], preferred_element_type=_F32), cwb_ref[c], cbb_ref[c])
        act = (a * jax.nn.sigmoid(a) * b).astype(MXU_DTYPE)
        acc_ref[...] += jnp.dot(act, wd_ref[c], preferred_element_type=_F32)
        return carry

    lax.fori_loop(0, n_chunks, chunk, 0)
    out_ref[...] = _rmsnorm(h + acc_ref[...], gf_ref[...])


def _conv_ffn(h, g, wa, wb, cwa, cwb, cba, cbb, wd, gf, tm):
    B, S, D = h.shape
    n_chunks = wa.shape[0]
    grid = (B, S // tm)
    const2 = lambda b, i: (0, 0)
    const3 = lambda b, i: (0, 0, 0)
    halo_blocks = tm // FFN_HALO
    return pl.pallas_call(
        functools.partial(_ffn_kernel, tm=tm, n_chunks=n_chunks),
        grid=grid,
        in_specs=[
            pl.BlockSpec((None, tm, D), lambda b, i: (b, i, 0)),
            pl.BlockSpec((None, FFN_HALO, D),
                         lambda b, i: (b, jnp.maximum(i * halo_blocks - 1, 0), 0)),
            pl.BlockSpec((1, D), const2),
            pl.BlockSpec(wa.shape, const3),
            pl.BlockSpec(wb.shape, const3),
            pl.BlockSpec(cwa.shape, const3),
            pl.BlockSpec(cwb.shape, const3),
            pl.BlockSpec(cba.shape, const3),
            pl.BlockSpec(cbb.shape, const3),
            pl.BlockSpec(wd.shape, const3),
            pl.BlockSpec((1, D), const2),
        ],
        out_specs=pl.BlockSpec((None, tm, D), lambda b, i: (b, i, 0)),
        out_shape=jax.ShapeDtypeStruct((B, S, D), _F32),
        scratch_shapes=[
            pltpu.VMEM((FFN_HALO + tm, D), MXU_DTYPE),
            pltpu.VMEM((FFN_HALO + tm, FF_CHUNK), _F32),
            pltpu.VMEM((FFN_HALO + tm, FF_CHUNK), _F32),
            pltpu.VMEM((tm, D), _F32),
        ],
        compiler_params=_params(),
        name="conv_ffn",
    )(h, h, g, wa, wb, cwa, cwb, cba, cbb, wd, gf)


def _rope_tables(S):
    half = ROT_DIM // 2
    inv_freq = 1.0 / (ROPE_THETA ** (jnp.arange(half, dtype=_F32) * 2.0 / ROT_DIM))
    ang = jnp.arange(S, dtype=_F32)[:, None] * inv_freq[None, :]
    cos, sin = jnp.cos(ang), jnp.sin(ang)
    zeros = jnp.zeros((S, HEAD_DIM - ROT_DIM), _F32)
    zh = jnp.zeros((S, half), _F32)
    ra = jnp.concatenate([cos, cos, jnp.ones_like(zeros)], axis=1)
    rb = jnp.concatenate([zh, sin, zeros], axis=1)
    rc = jnp.concatenate([-sin, zh, zeros], axis=1)
    rep = LANES // HEAD_DIM
    tile = lambda a: jnp.tile(a, (1, rep))
    return cos.T, sin.T, tile(ra), tile(rb), tile(rc)


def _layer(h, norm_mix_g, w_in, pool_w, pool_scale, w_pool_proj, w_attn_proj, w_out,
           norm_ffn_g, w_up, conv_w, conv_b, w_down, norm_out_g, tables):
    B, S, D = h.shape
    pool_width = D // 2
    attn_w = N_HEADS * HEAD_DIM
    idx_w = IDX_HEADS * IDX_DIM
    d_ff = w_down.shape[0]
    ktop = min(TOPK_MAX, S // 4)
    c0 = pool_width
    cq, ck, cv = c0, c0 + attn_w, c0 + 2 * attn_w
    ciq = c0 + 3 * attn_w
    cik = ciq + idx_w
    ciw = cik + IDX_DIM
    cg = ciw + IDX_HEADS
    cast = lambda a: a.astype(MXU_DTYPE)

    wrow = cast(jnp.concatenate(
        [w_in[:, ck:cv], w_in[:, cik:ciw], jnp.zeros((D, LANES - IDX_DIM), w_in.dtype)], axis=1))
    wt = cast(jnp.concatenate(
        [w_in[:, cq:ck], w_in[:, ciq:cik], w_in[:, cv:ciq], w_in[:, ciw:cg],
         jnp.zeros((D, 16 - IDX_HEADS), w_in.dtype)], axis=1).T)
    row1 = lambda a: a.reshape(1, -1)

    tm = TKF
    qT, iqT, vT, wT, k, ik = _projection(h, row1(norm_mix_g), wrow, wt, *tables, tm=tm)
    o = _attention(qT, iqT, vT, wT, k, ik, ktop)
    h1 = _mixer_out(h, o, row1(norm_mix_g), cast(w_in[:, :c0]), cast(w_in[:, cg:]),
                    cast(pool_w), row1(pool_scale), cast(w_pool_proj), cast(w_attn_proj),
                    cast(w_out), tm=TQ)

    n_chunks = d_ff // FF_CHUNK
    split_cols = lambda a: a.reshape(a.shape[0], n_chunks, FF_CHUNK).transpose(1, 0, 2)
    wa, wb = cast(split_cols(w_up[:, :d_ff])), cast(split_cols(w_up[:, d_ff:]))
    cwa, cwb = split_cols(conv_w[:, :d_ff]), split_cols(conv_w[:, d_ff:])
    cba, cbb = split_cols(conv_b[None, :d_ff]), split_cols(conv_b[None, d_ff:])
    wd = cast(w_down.reshape(n_chunks, FF_CHUNK, D))
    return _conv_ffn(h1, row1(norm_ffn_g), wa, wb, cwa, cwb, cba, cbb, wd, row1(norm_out_g), tm=TQ)


def kernel(x, norm_mix_g, w_in, pool_w, pool_scale, w_pool_proj, w_attn_proj, w_out, norm_ffn_g,
           w_up, conv_w, conv_b, w_down, norm_final_g):
    depth = w_in.shape[0]
    assert depth == 1, "the final RMSNorm is fused into the last layer's ConvFFN kernel"
    S = x.shape[1]
    assert S % TKF == 0 and TQ == TK and TKF % TK == 0
    tables = _rope_tables(S)
    return _layer(x, norm_mix_g[0], w_in[0], pool_w[0], pool_scale[0], w_pool_proj[0],
                  w_attn_proj[0], w_out[0], norm_ffn_g[0], w_up[0], conv_w[0], conv_b[0],
                  w_down[0], norm_final_g, tables)
```

```python
import functools
import math

import jax
import jax.numpy as jnp
from jax import lax
from jax.experimental import pallas as pl
from jax.experimental.pallas import tpu as pltpu

POOL_GROUPS = 4
POOL_WINDOWS = (2, 4, 8, 16)
N_HEADS = 8
HEAD_DIM = 64
ROT_DIM = HEAD_DIM // 4
ROPE_THETA = 500000.0
IDX_HEADS = 8
IDX_DIM = 64
TOPK_MAX = 256
N_BRANCH = 2
CONV_WIDTH = 3
EPS = 1e-6

LANES = 128
SUBLANES = 8
PACK_ROWS = 16
VMEM_LIMIT_BYTES = 56 * 1024 * 1024
MXU_DTYPE = jnp.bfloat16

TQ = 256
TK = 256
TKF = 512
SCORE_ROWS = 128
COUNT_ROWS = 256
LOWER_STEPS = 2
V_ROWS = HEAD_DIM + 16
LOG2E = 1.4426950408889634
POOL_HALO = 16
FFN_HALO = 16
FF_CHUNK = 256

INT_MIN = -(2 ** 31)
NEG_BIG = -1e30

_F32 = jnp.float32


def _rmsnorm(x, g):
    return x * lax.rsqrt(jnp.mean(x * x, axis=-1, keepdims=True) + EPS) * g


def _params(**kw):
    return pltpu.CompilerParams(vmem_limit_bytes=VMEM_LIMIT_BYTES, **kw)


def _resident(block_shape, index_map):
    return pl.BlockSpec(block_shape, index_map, pipeline_mode=pl.Buffered(1))


def _proj_kernel(x_ref, g_ref, wrow_ref, wt_ref, cos_ref, sin_ref, ra_ref, rb_ref, rc_ref,
                 qT_ref, iqT_ref, vT_ref, wT_ref, k_ref, ik_ref, *, n_vt):
    attn_w = N_HEADS * HEAD_DIM
    idx_w = IDX_HEADS * IDX_DIM
    xn = _rmsnorm(x_ref[...], g_ref[...]).astype(MXU_DTYPE)
    row = jnp.dot(xn, wrow_ref[...], preferred_element_type=_F32)
    tr = lax.dot_general(wt_ref[...], xn, (((1,), (1,)), ((), ())),
                         preferred_element_type=_F32)

    cos = cos_ref[...]
    sin = sin_ref[...]
    half = ROT_DIM // 2

    def rope_t(z, n_heads, dim):
        parts = []
        for h in range(n_heads):
            b = h * dim
            x1 = z[b:b + half]
            x2 = z[b + half:b + ROT_DIM]
            parts += [x1 * cos - x2 * sin, x2 * cos + x1 * sin, z[b + ROT_DIM:b + dim]]
        return jnp.concatenate(parts, axis=0)

    qT_ref[...] = (rope_t(tr[0:attn_w], N_HEADS, HEAD_DIM)
                   * (HEAD_DIM ** -0.5 * LOG2E)).astype(qT_ref.dtype)
    iqT_ref[...] = rope_t(tr[attn_w:attn_w + idx_w], IDX_HEADS, IDX_DIM).astype(iqT_ref.dtype)
    vt = tr[attn_w + idx_w:2 * attn_w + idx_w]
    ones_rows = (lax.broadcasted_iota(jnp.int32, (V_ROWS - HEAD_DIM, TKF), 0) == 0).astype(_F32)
    for j in range(n_vt):
        for h in range(N_HEADS):
            vT_ref[j, h, 0:HEAD_DIM, :] = vt[h * HEAD_DIM:(h + 1) * HEAD_DIM,
                                             j * TKF:(j + 1) * TKF].astype(vT_ref.dtype)
            vT_ref[j, h, HEAD_DIM:V_ROWS, :] = ones_rows.astype(vT_ref.dtype)
    wT_ref[...] = tr[2 * attn_w + idx_w:2 * attn_w + idx_w + IDX_HEADS] * (
        (IDX_HEADS ** -0.5) * (IDX_DIM ** -0.5))

    ra = ra_ref[...]
    rb = rb_ref[...]
    rc = rc_ref[...]

    def rope_rows(c):
        return (c * ra + pltpu.roll(c, half, 1) * rb + pltpu.roll(c, LANES - half, 1) * rc)

    for c in range(attn_w // LANES):
        k_ref[:, c * LANES:(c + 1) * LANES] = rope_rows(
            row[:, c * LANES:(c + 1) * LANES]).astype(k_ref.dtype)
    ik_ref[...] = rope_rows(row[:, attn_w:attn_w + LANES])[:, :IDX_DIM].astype(ik_ref.dtype)


def _projection(x, g, wrow, wt, cosT, sinT, ra, rb, rc, tm):
    B, S, D = x.shape
    attn_w = N_HEADS * HEAD_DIM
    idx_w = IDX_HEADS * IDX_DIM
    n_vt = tm // TKF
    grid = (B, S // tm)
    const = lambda b, i: (0, 0)
    out_shape = (
        jax.ShapeDtypeStruct((B, attn_w, S), MXU_DTYPE),
        jax.ShapeDtypeStruct((B, idx_w, S), MXU_DTYPE),
        jax.ShapeDtypeStruct((B, S // TKF, N_HEADS, V_ROWS, TKF), MXU_DTYPE),
        jax.ShapeDtypeStruct((B, IDX_HEADS, S), _F32),
        jax.ShapeDtypeStruct((B, S, attn_w), MXU_DTYPE),
        jax.ShapeDtypeStruct((B, S, IDX_DIM), MXU_DTYPE),
    )
    return pl.pallas_call(
        functools.partial(_proj_kernel, n_vt=n_vt),
        grid=grid,
        in_specs=[
            pl.BlockSpec((None, tm, D), lambda b, i: (b, i, 0)),
            pl.BlockSpec((1, D), const),
            pl.BlockSpec(wrow.shape, const),
            pl.BlockSpec(wt.shape, const),
            pl.BlockSpec((ROT_DIM // 2, tm), lambda b, i: (0, i)),
            pl.BlockSpec((ROT_DIM // 2, tm), lambda b, i: (0, i)),
            pl.BlockSpec((tm, LANES), lambda b, i: (i, 0)),
            pl.BlockSpec((tm, LANES), lambda b, i: (i, 0)),
            pl.BlockSpec((tm, LANES), lambda b, i: (i, 0)),
        ],
        out_specs=(
            pl.BlockSpec((None, attn_w, tm), lambda b, i: (b, 0, i)),
            pl.BlockSpec((None, idx_w, tm), lambda b, i: (b, 0, i)),
            pl.BlockSpec((None, n_vt, N_HEADS, V_ROWS, TKF), lambda b, i: (b, i, 0, 0, 0)),
            pl.BlockSpec((None, IDX_HEADS, tm), lambda b, i: (b, 0, i)),
            pl.BlockSpec((None, tm, attn_w), lambda b, i: (b, i, 0)),
            pl.BlockSpec((None, tm, IDX_DIM), lambda b, i: (b, i, 0)),
        ),
        out_shape=out_shape,
        compiler_params=_params(),
        name="proj_rope",
    )(x, g, wrow, wt, cosT, sinT, ra, rb, rc)


def _attn_kernel(iqT_ref, wT_ref, qT_ref, ik_ref, k_ref, vT_ref, o_ref,
                 sc_ref, hi_ref, qpad_ref, acc_ref, m_ref, alpha_ref, bias_ref, on_ref, *stage_refs,
                 ktop, idx_bits):
    s_refs, p_refs = stage_refs[:N_HEADS], stage_refs[N_HEADS:]
    i = pl.program_id(1)
    stage = 0
    n_rows = (i + 1) * TK

    sub_rows = lax.broadcasted_iota(jnp.int32, (SCORE_ROWS, TQ), 0)
    sub_cols = lax.broadcasted_iota(jnp.int32, (SCORE_ROWS, TQ), 1)

    def score_tile(kb, diagonal):
        k0 = pl.multiple_of(kb * TK, TK)
        for sub in range(TK // SCORE_ROWS):
            r0 = k0 + sub * SCORE_ROWS
            ikb = ik_ref[pl.ds(r0, SCORE_ROWS), :]
            score = None
            for h in range(IDX_HEADS):
                s = jnp.dot(ikb, iqT_ref[h * IDX_DIM:(h + 1) * IDX_DIM, :],
                            preferred_element_type=_F32)
                term = jnp.maximum(s, 0.0) * wT_ref[h:h + 1, :]
                score = term if score is None else score + term
            score = jnp.where(score == 0.0, 0.0, score)
            bits = pltpu.bitcast(score, jnp.int32)
            key = jnp.where(bits < 0, bits ^ jnp.int32(0x7FFFFFFF), bits)
            if diagonal:
                causal = sub_rows + sub * SCORE_ROWS <= sub_cols
                key = jnp.where(causal, key, jnp.int32(INT_MIN))
            sc_ref[pl.ds(r0, SCORE_ROWS), :] = key
            hi_ref[pl.ds(r0, SCORE_ROWS), :] = jnp.right_shift(key, 16).astype(hi_ref.dtype)

    def score_body(kb, carry):
        score_tile(kb, False)
        return carry

    lax.fori_loop(0, i, score_body, 0)
    score_tile(i, True)

    row8_iota = lax.broadcasted_iota(jnp.int32, (SUBLANES, TQ), 0)

    def count(pred):
        def body(r, cnt):
            r0 = pl.multiple_of(r * COUNT_ROWS, COUNT_ROWS)
            blk = sc_ref[pl.ds(r0, COUNT_ROWS), :]
            parts = []
            for j in range(COUNT_ROWS // SUBLANES):
                parts.append(pred(blk[j * SUBLANES:(j + 1) * SUBLANES],
                                  r0 + j * SUBLANES + row8_iota).astype(jnp.int32))
            while len(parts) > 1:
                parts = [parts[a] + parts[a + 1] for a in range(0, len(parts), 2)]
            return cnt + parts[0]
        cnt = lax.fori_loop(0, n_rows // COUNT_ROWS, body, jnp.zeros((SUBLANES, TQ), jnp.int32))
        return jnp.sum(cnt.astype(_F32), axis=0, keepdims=True)

    def step(tb, cge, crej, cand_b, cnt):
        ok = cnt >= ktop
        return (jnp.where(ok, cand_b, tb), jnp.where(ok, cnt, cge), jnp.where(ok, crej, cnt))

    def count_upper(cand_hi):
        cand16 = jnp.broadcast_to(cand_hi, (PACK_ROWS, TQ)).astype(hi_ref.dtype)
        one16 = jnp.ones((PACK_ROWS, TQ), hi_ref.dtype)
        zero16 = jnp.zeros((PACK_ROWS, TQ), hi_ref.dtype)

        def body(r, cnt):
            r0 = pl.multiple_of(r * COUNT_ROWS, COUNT_ROWS)
            blk = hi_ref[pl.ds(r0, COUNT_ROWS), :]
            parts = [jnp.where(blk[j * PACK_ROWS:(j + 1) * PACK_ROWS] >= cand16, one16, zero16)
                     for j in range(COUNT_ROWS // PACK_ROWS)]
            while len(parts) > 1:
                parts = [parts[a] + parts[a + 1] for a in range(0, len(parts), 2)]
            return cnt + parts[0].astype(jnp.int32)
        cnt = lax.fori_loop(0, n_rows // COUNT_ROWS, body, jnp.zeros((PACK_ROWS, TQ), jnp.int32))
        return jnp.sum(cnt.astype(_F32), axis=0, keepdims=True)

    def upper_body(it, carry):
        tb = carry[0]
        cand_b = tb | jnp.left_shift(jnp.int32(1), 31 - it)
        cand = cand_b ^ jnp.int32(INT_MIN)
        return step(*carry, cand_b, count_upper(jnp.right_shift(cand, 16)))

    zeros_q = jnp.zeros((1, TQ), jnp.int32)
    zeros_f = jnp.zeros((1, TQ), _F32)
    tb, cge, crej = lax.fori_loop(0, 16, upper_body, (zeros_q, zeros_f, zeros_f))

    lane = lax.broadcasted_iota(jnp.int32, (1, TQ), 1)
    enough_keys = i * TQ + lane + 1 >= ktop

    def unsettled(cge):
        return jnp.max(jnp.where(enough_keys & (cge != ktop), 1.0, 0.0))

    def lower_cond(carry):
        return (carry[0] < 32) & (carry[4] > 0.0)

    def lower_body(carry):
        it, tb, cge, crej, _ = carry
        for _ in range(LOWER_STEPS):
            cand_b = tb | jnp.left_shift(jnp.int32(1), 31 - it)
            cand = cand_b ^ jnp.int32(INT_MIN)
            tb, cge, crej = step(tb, cge, crej, cand_b, count(lambda blk, idx: blk >= cand))
            it = it + 1
        return it, tb, cge, crej, unsettled(cge)

    _, tb, cge, crej, _ = lax.while_loop(lower_cond, lower_body,
                                         (jnp.int32(16), tb, cge, crej, unsettled(cge)))
    thr = tb ^ jnp.int32(INT_MIN)
    has_thr = tb != 0

    need = ktop - crej
    any_tie = jnp.max(jnp.where(has_thr & (cge > ktop), 1.0, 0.0)) > 0.0

    @pl.when(any_tie)
    def _demote_surplus_ties():
        def body(it, p):
            cand = p | jnp.left_shift(jnp.int32(1), idx_bits - 1 - it)
            cnt = count(lambda blk, idx: (blk == thr) & (idx < cand))
            return jnp.where(cnt < need, cand, p)
        tie_last = lax.fori_loop(0, idx_bits, body, zeros_q)
        blk_iota = lax.broadcasted_iota(jnp.int32, (COUNT_ROWS, TQ), 0)

        def demote(r, carry):
            r0 = pl.multiple_of(r * COUNT_ROWS, COUNT_ROWS)
            blk = sc_ref[pl.ds(r0, COUNT_ROWS), :]
            surplus = has_thr & (blk == thr) & ((r0 + blk_iota) > tie_last)
            sc_ref[pl.ds(r0, COUNT_ROWS), :] = jnp.where(surplus, thr - 1, blk)
            return carry
        lax.fori_loop(0, n_rows // COUNT_ROWS, demote, 0)

    thr_ge = jnp.where(has_thr, thr, jnp.int32(INT_MIN + 1))

    pair_rows = lax.broadcasted_iota(jnp.int32, (2 * HEAD_DIM, TQ), 0)
    for h in range(N_HEADS):
        pair = qT_ref[(h // 2) * 2 * HEAD_DIM:(h // 2 + 1) * 2 * HEAD_DIM, :].astype(_F32)
        mine = (pair_rows >= HEAD_DIM) if h % 2 else (pair_rows < HEAD_DIM)
        qpad_ref[h] = jnp.where(mine, pair, 0.0).astype(qpad_ref.dtype)
    m_ref[...] = jnp.full(m_ref.shape, NEG_BIG, _F32)
    acc_ref[...] = jnp.zeros(acc_ref.shape, _F32)

    @pl.when((i + 1) % (TKF // TK) != 0)
    def _pad_keys():
        sc_ref[pl.ds(pl.multiple_of(n_rows, TK), TK), :] = jnp.full((TK, TQ), INT_MIN, jnp.int32)

    def flash_body(kb, carry):
        k0 = pl.multiple_of(kb * TKF, TKF)
        bias_ref[...] = jnp.where(sc_ref[pl.ds(k0, TKF), :] >= thr_ge, 0.0, NEG_BIG)
        for h in range(N_HEADS):
            kpair = k_ref[pl.ds(k0, TKF), (h // 2) * 2 * HEAD_DIM:(h // 2 + 1) * 2 * HEAD_DIM]
            s_refs[h][0] = jnp.dot(kpair, qpad_ref[h], preferred_element_type=_F32)
        for h in range(N_HEADS):
            col_max = [None] * 4
            for j in range(TKF // SUBLANES):
                rows = slice(j * SUBLANES, (j + 1) * SUBLANES)
                slab = s_refs[h][stage, rows, :] + bias_ref[rows, :]
                c = j % len(col_max)
                col_max[c] = slab if col_max[c] is None else jnp.maximum(col_max[c], slab)
            tile_max = jnp.maximum(jnp.maximum(col_max[0], col_max[1]),
                                   jnp.maximum(col_max[2], col_max[3]))
            m_old = m_ref[h:h + 1, :]
            m_new = jnp.maximum(m_old, jnp.max(tile_max, axis=0, keepdims=True))
            alpha_ref[h:h + 1, :] = jnp.exp2(m_old - m_new)
            m_ref[h:h + 1, :] = m_new
        for h in range(N_HEADS):
            p_refs[h][0] = jnp.exp2(s_refs[h][stage] + bias_ref[...]
                                    - m_ref[h:h + 1, :]).astype(p_refs[h].dtype)
        for h in range(N_HEADS):
            pv = jnp.dot(vT_ref[kb, h], p_refs[h][stage], preferred_element_type=_F32)
            acc_ref[h] = alpha_ref[h:h + 1, :] * acc_ref[h] + pv
        return carry

    lax.fori_loop(0, (i + TKF // TK) // (TKF // TK), flash_body, 0)

    for h in range(N_HEADS):
        on_ref[h * HEAD_DIM:(h + 1) * HEAD_DIM, :] = (
            acc_ref[h, 0:HEAD_DIM, :] / acc_ref[h, HEAD_DIM:HEAD_DIM + 1, :])
    o_ref[...] = on_ref[...].T.astype(o_ref.dtype)


def _attention(qT, iqT, vT, wT, k, ik, ktop):
    B, attn_w, S = qT.shape
    idx_w = iqT.shape[1]
    idx_bits = max(1, math.ceil(math.log2(S)))
    grid = (B, S // TQ)
    return pl.pallas_call(
        functools.partial(_attn_kernel, ktop=ktop, idx_bits=idx_bits),
        grid=grid,
        in_specs=[
            pl.BlockSpec((None, idx_w, TQ), lambda b, i: (b, 0, i)),
            pl.BlockSpec((None, IDX_HEADS, TQ), lambda b, i: (b, 0, i)),
            pl.BlockSpec((None, attn_w, TQ), lambda b, i: (b, 0, i)),
            _resident((None, S, IDX_DIM), lambda b, i: (b, 0, 0)),
            _resident((None, S, attn_w), lambda b, i: (b, 0, 0)),
            _resident((None, S // TKF, N_HEADS, V_ROWS, TKF), lambda b, i: (b, 0, 0, 0, 0)),
        ],
        out_specs=pl.BlockSpec((None, TQ, attn_w), lambda b, i: (b, i, 0)),
        out_shape=jax.ShapeDtypeStruct((B, S, attn_w), MXU_DTYPE),
        scratch_shapes=[
            pltpu.VMEM((S, TQ), jnp.int32),
            pltpu.VMEM((S, TQ), jnp.int16),
            pltpu.VMEM((N_HEADS, 2 * HEAD_DIM, TQ), MXU_DTYPE),
            pltpu.VMEM((N_HEADS, V_ROWS, TQ), _F32),
            pltpu.VMEM((N_HEADS, TQ), _F32),
            pltpu.VMEM((N_HEADS, TQ), _F32),
            pltpu.VMEM((TKF, TQ), _F32),
            pltpu.VMEM((attn_w, TQ), _F32),
        ] + [pltpu.VMEM((1, TKF, TQ), _F32)] * N_HEADS
          + [pltpu.VMEM((1, TKF, TQ), MXU_DTYPE)] * N_HEADS,
        compiler_params=_params(),
        name="topk_attention",
    )(iqT, wT, qT, ik, k, vT)


def _mixer_kernel(x_ref, halo_ref, o_ref, g_ref, wpool_ref, wgate_ref, poolw_ref, pscale_ref,
                  wpp_ref, wap_ref, wout_ref, h_ref, u_ref, *, tm):
    i = pl.program_id(1)
    D = x_ref.shape[-1]
    group_dim = wpool_ref.shape[1] // POOL_GROUPS
    g = g_ref[...]
    x = x_ref[...]
    xn = _rmsnorm(x, g).astype(MXU_DTYPE)
    hn = _rmsnorm(halo_ref[...], g).astype(MXU_DTYPE)
    u_halo = jnp.dot(hn, wpool_ref[...], preferred_element_type=_F32)
    u_ref[0:POOL_HALO, :] = jnp.where(i > 0, u_halo, 0.0)
    u_ref[POOL_HALO:POOL_HALO + tm, :] = jnp.dot(xn, wpool_ref[...], preferred_element_type=_F32)

    t = i * tm + lax.broadcasted_iota(jnp.int32, (tm, group_dim), 0)
    mixed = []
    for gi, w in enumerate(POOL_WINDOWS):
        lanes = slice(gi * group_dim, (gi + 1) * group_dim)
        cur = u_ref[POOL_HALO:POOL_HALO + tm, lanes]
        sums = cur
        for j in range(1, w):
            sums = sums + u_ref[POOL_HALO - j:POOL_HALO - j + tm, lanes]
        cnt = jnp.minimum(t + 1, w).astype(_F32)
        pooled = sums / cnt - cur
        mixed.append(jnp.dot(pooled.astype(MXU_DTYPE), poolw_ref[gi], preferred_element_type=_F32))
    mixed = jnp.concatenate(mixed, axis=-1) * pscale_ref[...]
    y_pool = jnp.dot(mixed.astype(MXU_DTYPE), wpp_ref[...], preferred_element_type=_F32)
    y_attn = jnp.dot(o_ref[...], wap_ref[...], preferred_element_type=_F32)
    gates = jax.nn.sigmoid(jnp.dot(xn, wgate_ref[...], preferred_element_type=_F32))
    merged = gates[:, :D] * y_pool + gates[:, D:] * y_attn
    h_ref[...] = x + jnp.dot(merged.astype(MXU_DTYPE), wout_ref[...], preferred_element_type=_F32)


def _mixer_out(x, o, g, wpool, wgate, poolw, pscale, wpp, wap, wout, tm):
    B, S, D = x.shape
    pool_w = wpool.shape[1]
    attn_w = o.shape[-1]
    grid = (B, S // tm)
    const2 = lambda b, i: (0, 0)
    halo_blocks = tm // POOL_HALO
    return pl.pallas_call(
        functools.partial(_mixer_kernel, tm=tm),
        grid=grid,
        in_specs=[
            pl.BlockSpec((None, tm, D), lambda b, i: (b, i, 0)),
            pl.BlockSpec((None, POOL_HALO, D),
                         lambda b, i: (b, jnp.maximum(i * halo_blocks - 1, 0), 0)),
            pl.BlockSpec((None, tm, attn_w), lambda b, i: (b, i, 0)),
            pl.BlockSpec((1, D), const2),
            pl.BlockSpec(wpool.shape, const2),
            pl.BlockSpec(wgate.shape, const2),
            pl.BlockSpec(poolw.shape, lambda b, i: (0, 0, 0)),
            pl.BlockSpec((1, pool_w), const2),
            pl.BlockSpec(wpp.shape, const2),
            pl.BlockSpec(wap.shape, const2),
            pl.BlockSpec(wout.shape, const2),
        ],
        out_specs=pl.BlockSpec((None, tm, D), lambda b, i: (b, i, 0)),
        out_shape=jax.ShapeDtypeStruct((B, S, D), _F32),
        scratch_shapes=[pltpu.VMEM((POOL_HALO + tm, pool_w), _F32)],
        compiler_params=_params(),
        name="mixer_out",
    )(x, x, o, g, wpool, wgate, poolw, pscale, wpp, wap, wout)


def _ffn_kernel(h_ref, halo_ref, g_ref, wa_ref, wb_ref, cwa_ref, cwb_ref, cba_ref, cbb_ref,
                wd_ref, gf_ref, out_ref, hn_ref, acc_ref, ua0_ref, ua1_ref, ub0_ref, ub1_ref,
                *, tm, n_chunks):
    ua_refs, ub_refs = (ua0_ref, ua1_ref), (ub0_ref, ub1_ref)
    i = pl.program_id(1)
    g = g_ref[...]
    h = h_ref[...]
    hn_ref[0:FFN_HALO, :] = _rmsnorm(halo_ref[...], g).astype(MXU_DTYPE)
    hn_ref[FFN_HALO:FFN_HALO + tm, :] = _rmsnorm(h, g).astype(MXU_DTYPE)
    acc_ref[...] = jnp.zeros(acc_ref.shape, _F32)
    first = i == 0

    def conv(u_ref, up, cw, cb):
        u_ref[FFN_HALO:FFN_HALO + tm, :] = up[FFN_HALO:, :]
        u_ref[0:FFN_HALO, :] = jnp.where(first, 0.0, up[0:FFN_HALO, :])
        out = cb
        for j in range(CONV_WIDTH):
            off = FFN_HALO - (CONV_WIDTH - 1) + j
            out = out + u_ref[off:off + tm, :] * cw[j:j + 1, :]
        return out

    for c in range(n_chunks):
        hn = hn_ref[...]
        a = conv(ua_refs[c % 2], jnp.dot(hn, wa_ref[c], preferred_element_type=_F32),
                 cwa_ref[c], cba_ref[c])
        b = conv(ub_refs[c % 2], jnp.dot(hn, wb_ref[c], preferred_element_type=_F32),
                 cwb_ref[c], cbb_ref[c])
        act = (a * jax.nn.sigmoid(a) * b).astype(MXU_DTYPE)
        acc_ref[...] += jnp.dot(act, wd_ref[c], preferred_element_type=_F32)
    out_ref[...] = _rmsnorm(h + acc_ref[...], gf_ref[...])


def _conv_ffn(h, g, wa, wb, cwa, cwb, cba, cbb, wd, gf, tm):
    B, S, D = h.shape
    n_chunks = wa.shape[0]
    grid = (B, S // tm)
    const2 = lambda b, i: (0, 0)
    const3 = lambda b, i: (0, 0, 0)
    halo_blocks = tm // FFN_HALO
    return pl.pallas_call(
        functools.partial(_ffn_kernel, tm=tm, n_chunks=n_chunks),
        grid=grid,
        in_specs=[
            pl.BlockSpec((None, tm, D), lambda b, i: (b, i, 0)),
            pl.BlockSpec((None, FFN_HALO, D),
                         lambda b, i: (b, jnp.maximum(i * halo_blocks - 1, 0), 0)),
            pl.BlockSpec((1, D), const2),
            _resident(wa.shape, const3),
            _resident(wb.shape, const3),
            pl.BlockSpec(cwa.shape, const3),
            pl.BlockSpec(cwb.shape, const3),
            pl.BlockSpec(cba.shape, const3),
            pl.BlockSpec(cbb.shape, const3),
            _resident(wd.shape, const3),
            pl.BlockSpec((1, D), const2),
        ],
        out_specs=pl.BlockSpec((None, tm, D), lambda b, i: (b, i, 0)),
        out_shape=jax.ShapeDtypeStruct((B, S, D), _F32),
        scratch_shapes=[
            pltpu.VMEM((FFN_HALO + tm, D), MXU_DTYPE),
            pltpu.VMEM((tm, D), _F32),
        ] + [pltpu.VMEM((FFN_HALO + tm, FF_CHUNK), _F32)] * 4,
        compiler_params=_params(),
        name="conv_ffn",
    )(h, h, g, wa, wb, cwa, cwb, cba, cbb, wd, gf)


def _rope_tables(S):
    half = ROT_DIM // 2
    inv_freq = 1.0 / (ROPE_THETA ** (jnp.arange(half, dtype=_F32) * 2.0 / ROT_DIM))
    ang = jnp.arange(S, dtype=_F32)[:, None] * inv_freq[None, :]
    cos, sin = jnp.cos(ang), jnp.sin(ang)
    zeros = jnp.zeros((S, HEAD_DIM - ROT_DIM), _F32)
    zh = jnp.zeros((S, half), _F32)
    ra = jnp.concatenate([cos, cos, jnp.ones_like(zeros)], axis=1)
    rb = jnp.concatenate([zh, sin, zeros], axis=1)
    rc = jnp.concatenate([-sin, zh, zeros], axis=1)
    rep = LANES // HEAD_DIM
    tile = lambda a: jnp.tile(a, (1, rep))
    return cos.T, sin.T, tile(ra), tile(rb), tile(rc)


def _layer(h, norm_mix_g, w_in, pool_w, pool_scale, w_pool_proj, w_attn_proj, w_out,
           norm_ffn_g, w_up, conv_w, conv_b, w_down, norm_out_g, tables):
    B, S, D = h.shape
    pool_width = D // 2
    attn_w = N_HEADS * HEAD_DIM
    idx_w = IDX_HEADS * IDX_DIM
    d_ff = w_down.shape[0]
    ktop = min(TOPK_MAX, S // 4)
    c0 = pool_width
    cq, ck, cv = c0, c0 + attn_w, c0 + 2 * attn_w
    ciq = c0 + 3 * attn_w
    cik = ciq + idx_w
    ciw = cik + IDX_DIM
    cg = ciw + IDX_HEADS
    cast = lambda a: a.astype(MXU_DTYPE)

    wrow = cast(jnp.concatenate(
        [w_in[:, ck:cv], w_in[:, cik:ciw], jnp.zeros((D, LANES - IDX_DIM), w_in.dtype)], axis=1))
    wt = cast(jnp.concatenate(
        [w_in[:, cq:ck], w_in[:, ciq:cik], w_in[:, cv:ciq], w_in[:, ciw:cg],
         jnp.zeros((D, 16 - IDX_HEADS), w_in.dtype)], axis=1).T)
    row1 = lambda a: a.reshape(1, -1)

    tm = TKF
    qT, iqT, vT, wT, k, ik = _projection(h, row1(norm_mix_g), wrow, wt, *tables, tm=tm)
    o = _attention(qT, iqT, vT, wT, k, ik, ktop)
    h1 = _mixer_out(h, o, row1(norm_mix_g), cast(w_in[:, :c0]), cast(w_in[:, cg:]),
                    cast(pool_w), row1(pool_scale), cast(w_pool_proj), cast(w_attn_proj),
                    cast(w_out), tm=TQ)

    n_chunks = d_ff // FF_CHUNK
    split_cols = lambda a: a.reshape(a.shape[0], n_chunks, FF_CHUNK).transpose(1, 0, 2)
    wa, wb = cast(split_cols(w_up[:, :d_ff])), cast(split_cols(w_up[:, d_ff:]))
    cwa, cwb = split_cols(conv_w[:, :d_ff]), split_cols(conv_w[:, d_ff:])
    cba, cbb = split_cols(conv_b[None, :d_ff]), split_cols(conv_b[None, d_ff:])
    wd = cast(w_down.reshape(n_chunks, FF_CHUNK, D))
    return _conv_ffn(h1, row1(norm_ffn_g), wa, wb, cwa, cwb, cba, cbb, wd, row1(norm_out_g), tm=tm)


def kernel(x, norm_mix_g, w_in, pool_w, pool_scale, w_pool_proj, w_attn_proj, w_out, norm_ffn_g,
           w_up, conv_w, conv_b, w_down, norm_final_g):
    depth = w_in.shape[0]
    assert depth == 1, "the final RMSNorm is fused into the last layer's ConvFFN kernel"
    S = x.shape[1]
    assert S % TKF == 0 and TQ == TK and TKF % TK == 0
    tables = _rope_tables(S)
    return _layer(x, norm_mix_g[0], w_in[0], pool_w[0], pool_scale[0], w_pool_proj[0],
                  w_attn_proj[0], w_out[0], norm_ffn_g[0], w_up[0], conv_w[0], conv_b[0],
                  w_down[0], norm_final_g, tables)
```

```python
import functools
import math

import jax
import jax.numpy as jnp
from jax import lax
from jax.experimental import pallas as pl
from jax.experimental.pallas import tpu as pltpu

POOL_GROUPS = 4
POOL_WINDOWS = (2, 4, 8, 16)
N_HEADS = 8
HEAD_DIM = 64
ROT_DIM = HEAD_DIM // 4
ROPE_THETA = 500000.0
IDX_HEADS = 8
IDX_DIM = 64
TOPK_MAX = 256
N_BRANCH = 2
CONV_WIDTH = 3
EPS = 1e-6

LANES = 128
SUBLANES = 8
PACK_ROWS = 16
VMEM_LIMIT_BYTES = 56 * 1024 * 1024
MXU_DTYPE = jnp.bfloat16

TQ = 256
TK = 256
TKF = 512
QK_ROWS = 256
SCORE_ROWS = 64
COUNT_ROWS = 256
LOWER_STEPS = 2
V_ROWS = HEAD_DIM + 16
LOG2E = 1.4426950408889634
POOL_HALO = 16
FFN_HALO = 16
FF_CHUNK = 256

INT_MIN = -(2 ** 31)
NEG_BIG = -1e30

_F32 = jnp.float32


def _rmsnorm(x, g):
    return x * lax.rsqrt(jnp.mean(x * x, axis=-1, keepdims=True) + EPS) * g


def _params(**kw):
    return pltpu.CompilerParams(vmem_limit_bytes=VMEM_LIMIT_BYTES, **kw)


def _resident(block_shape, index_map):
    return pl.BlockSpec(block_shape, index_map, pipeline_mode=pl.Buffered(1))


def _proj_kernel(x_ref, g_ref, wrow_ref, wt_ref, cos_ref, sin_ref, ra_ref, rb_ref, rc_ref,
                 qT_ref, iqT_ref, vT_ref, wT_ref, k_ref, ik_ref, *, n_vt):
    attn_w = N_HEADS * HEAD_DIM
    idx_w = IDX_HEADS * IDX_DIM
    xn = _rmsnorm(x_ref[...], g_ref[...]).astype(MXU_DTYPE)
    row = jnp.dot(xn, wrow_ref[...], preferred_element_type=_F32)
    tr = lax.dot_general(wt_ref[...], xn, (((1,), (1,)), ((), ())),
                         preferred_element_type=_F32)

    cos = cos_ref[...]
    sin = sin_ref[...]
    half = ROT_DIM // 2

    def rope_t(z, n_heads, dim):
        parts = []
        for h in range(n_heads):
            b = h * dim
            x1 = z[b:b + half]
            x2 = z[b + half:b + ROT_DIM]
            parts += [x1 * cos - x2 * sin, x2 * cos + x1 * sin, z[b + ROT_DIM:b + dim]]
        return jnp.concatenate(parts, axis=0)

    qT_ref[...] = (rope_t(tr[0:attn_w], N_HEADS, HEAD_DIM)
                   * (HEAD_DIM ** -0.5 * LOG2E)).astype(qT_ref.dtype)
    iqT_ref[...] = rope_t(tr[attn_w:attn_w + idx_w], IDX_HEADS, IDX_DIM).astype(iqT_ref.dtype)
    vt = tr[attn_w + idx_w:2 * attn_w + idx_w]
    ones_rows = (lax.broadcasted_iota(jnp.int32, (V_ROWS - HEAD_DIM, TKF), 0) == 0).astype(_F32)
    for j in range(n_vt):
        for h in range(N_HEADS):
            vT_ref[j, h, 0:HEAD_DIM, :] = vt[h * HEAD_DIM:(h + 1) * HEAD_DIM,
                                             j * TKF:(j + 1) * TKF].astype(vT_ref.dtype)
            vT_ref[j, h, HEAD_DIM:V_ROWS, :] = ones_rows.astype(vT_ref.dtype)
    wT_ref[...] = tr[2 * attn_w + idx_w:2 * attn_w + idx_w + IDX_HEADS] * (
        (IDX_HEADS ** -0.5) * (IDX_DIM ** -0.5))

    ra = ra_ref[...]
    rb = rb_ref[...]
    rc = rc_ref[...]

    def rope_rows(c):
        return (c * ra + pltpu.roll(c, half, 1) * rb + pltpu.roll(c, LANES - half, 1) * rc)

    for c in range(attn_w // LANES):
        k_ref[:, c * LANES:(c + 1) * LANES] = rope_rows(
            row[:, c * LANES:(c + 1) * LANES]).astype(k_ref.dtype)
    ik_ref[...] = rope_rows(row[:, attn_w:attn_w + LANES])[:, :IDX_DIM].astype(ik_ref.dtype)


def _projection(x, g, wrow, wt, cosT, sinT, ra, rb, rc, tm):
    B, S, D = x.shape
    attn_w = N_HEADS * HEAD_DIM
    idx_w = IDX_HEADS * IDX_DIM
    n_vt = tm // TKF
    grid = (B, S // tm)
    const = lambda b, i: (0, 0)
    out_shape = (
        jax.ShapeDtypeStruct((B, attn_w, S), MXU_DTYPE),
        jax.ShapeDtypeStruct((B, idx_w, S), MXU_DTYPE),
        jax.ShapeDtypeStruct((B, S // TKF, N_HEADS, V_ROWS, TKF), MXU_DTYPE),
        jax.ShapeDtypeStruct((B, IDX_HEADS, S), _F32),
        jax.ShapeDtypeStruct((B, S, attn_w), MXU_DTYPE),
        jax.ShapeDtypeStruct((B, S, IDX_DIM), MXU_DTYPE),
    )
    return pl.pallas_call(
        functools.partial(_proj_kernel, n_vt=n_vt),
        grid=grid,
        in_specs=[
            pl.BlockSpec((None, tm, D), lambda b, i: (b, i, 0)),
            pl.BlockSpec((1, D), const),
            pl.BlockSpec(wrow.shape, const),
            pl.BlockSpec(wt.shape, const),
            pl.BlockSpec((ROT_DIM // 2, tm), lambda b, i: (0, i)),
            pl.BlockSpec((ROT_DIM // 2, tm), lambda b, i: (0, i)),
            pl.BlockSpec((tm, LANES), lambda b, i: (i, 0)),
            pl.BlockSpec((tm, LANES), lambda b, i: (i, 0)),
            pl.BlockSpec((tm, LANES), lambda b, i: (i, 0)),
        ],
        out_specs=(
            pl.BlockSpec((None, attn_w, tm), lambda b, i: (b, 0, i)),
            pl.BlockSpec((None, idx_w, tm), lambda b, i: (b, 0, i)),
            pl.BlockSpec((None, n_vt, N_HEADS, V_ROWS, TKF), lambda b, i: (b, i, 0, 0, 0)),
            pl.BlockSpec((None, IDX_HEADS, tm), lambda b, i: (b, 0, i)),
            pl.BlockSpec((None, tm, attn_w), lambda b, i: (b, i, 0)),
            pl.BlockSpec((None, tm, IDX_DIM), lambda b, i: (b, i, 0)),
        ),
        out_shape=out_shape,
        compiler_params=_params(),
        name="proj_rope",
    )(x, g, wrow, wt, cosT, sinT, ra, rb, rc)


def _attn_kernel(iqT_ref, wT_ref, qT_ref, ik_ref, k_ref, vT_ref, o_ref,
                 sc_ref, hi_ref, qpad_ref, acc_ref, m_ref, alpha_ref, bias_ref, on_ref, *stage_refs,
                 ktop, idx_bits):
    s_refs, p_refs = stage_refs[:N_HEADS], stage_refs[N_HEADS:]
    i = pl.program_id(1)
    stage = 0
    n_rows = (i + 1) * TK

    sub_rows = lax.broadcasted_iota(jnp.int32, (SCORE_ROWS, TQ), 0)
    sub_cols = lax.broadcasted_iota(jnp.int32, (SCORE_ROWS, TQ), 1)

    def score_tile(kb, diagonal):
        k0 = pl.multiple_of(kb * TK, TK)
        for sub in range(TK // SCORE_ROWS):
            r0 = k0 + sub * SCORE_ROWS
            ikb = ik_ref[pl.ds(r0, SCORE_ROWS), :]
            score = None
            for h in range(IDX_HEADS):
                s = jnp.dot(ikb, iqT_ref[h * IDX_DIM:(h + 1) * IDX_DIM, :],
                            preferred_element_type=_F32)
                term = jnp.maximum(s, 0.0) * wT_ref[h:h + 1, :]
                score = term if score is None else score + term
            score = jnp.where(score == 0.0, 0.0, score)
            bits = pltpu.bitcast(score, jnp.int32)
            key = jnp.where(bits < 0, bits ^ jnp.int32(0x7FFFFFFF), bits)
            if diagonal:
                causal = sub_rows + sub * SCORE_ROWS <= sub_cols
                key = jnp.where(causal, key, jnp.int32(INT_MIN))
            sc_ref[pl.ds(r0, SCORE_ROWS), :] = key
            hi_ref[pl.ds(r0, SCORE_ROWS), :] = jnp.right_shift(key, 16).astype(hi_ref.dtype)

    def score_body(kb, carry):
        score_tile(kb, False)
        return carry

    lax.fori_loop(0, i, score_body, 0)
    score_tile(i, True)

    row8_iota = lax.broadcasted_iota(jnp.int32, (SUBLANES, TQ), 0)

    def count(pred):
        def body(r, cnt):
            r0 = pl.multiple_of(r * COUNT_ROWS, COUNT_ROWS)
            blk = sc_ref[pl.ds(r0, COUNT_ROWS), :]
            parts = []
            for j in range(COUNT_ROWS // SUBLANES):
                parts.append(pred(blk[j * SUBLANES:(j + 1) * SUBLANES],
                                  r0 + j * SUBLANES + row8_iota).astype(jnp.int32))
            while len(parts) > 1:
                parts = [parts[a] + parts[a + 1] for a in range(0, len(parts), 2)]
            return cnt + parts[0]
        cnt = lax.fori_loop(0, n_rows // COUNT_ROWS, body, jnp.zeros((SUBLANES, TQ), jnp.int32))
        return jnp.sum(cnt.astype(_F32), axis=0, keepdims=True)

    def step(tb, cge, crej, cand_b, cnt):
        ok = cnt >= ktop
        return (jnp.where(ok, cand_b, tb), jnp.where(ok, cnt, cge), jnp.where(ok, crej, cnt))

    def count_upper(cand_hi):
        cand16 = jnp.broadcast_to(cand_hi, (PACK_ROWS, TQ)).astype(hi_ref.dtype)
        one16 = jnp.ones((PACK_ROWS, TQ), hi_ref.dtype)
        zero16 = jnp.zeros((PACK_ROWS, TQ), hi_ref.dtype)

        def body(r, cnt):
            r0 = pl.multiple_of(r * COUNT_ROWS, COUNT_ROWS)
            blk = hi_ref[pl.ds(r0, COUNT_ROWS), :]
            parts = [jnp.where(blk[j * PACK_ROWS:(j + 1) * PACK_ROWS] >= cand16, one16, zero16)
                     for j in range(COUNT_ROWS // PACK_ROWS)]
            while len(parts) > 1:
                parts = [parts[a] + parts[a + 1] for a in range(0, len(parts), 2)]
            return cnt + parts[0].astype(jnp.int32)
        cnt = lax.fori_loop(0, n_rows // COUNT_ROWS, body, jnp.zeros((PACK_ROWS, TQ), jnp.int32))
        return jnp.sum(cnt.astype(_F32), axis=0, keepdims=True)

    def upper_body(it, carry):
        tb = carry[0]
        cand_b = tb | jnp.left_shift(jnp.int32(1), 31 - it)
        cand = cand_b ^ jnp.int32(INT_MIN)
        return step(*carry, cand_b, count_upper(jnp.right_shift(cand, 16)))

    zeros_q = jnp.zeros((1, TQ), jnp.int32)
    zeros_f = jnp.zeros((1, TQ), _F32)
    tb, cge, crej = lax.fori_loop(0, 16, upper_body, (zeros_q, zeros_f, zeros_f))

    above = crej
    thr_hi = jnp.right_shift(tb ^ jnp.int32(INT_MIN), 16)
    half_lo = -(1 << 15)

    def lower_plane(r, carry):
        r0 = pl.multiple_of(r * COUNT_ROWS, COUNT_ROWS)
        key = sc_ref[pl.ds(r0, COUNT_ROWS), :]
        low = jnp.where(jnp.right_shift(key, 16) == thr_hi, (key & 0xFFFF) + half_lo, half_lo)
        hi_ref[pl.ds(r0, COUNT_ROWS), :] = low.astype(hi_ref.dtype)
        return carry

    lax.fori_loop(0, n_rows // COUNT_ROWS, lower_plane, 0)

    lane = lax.broadcasted_iota(jnp.int32, (1, TQ), 1)
    enough_keys = i * TQ + lane + 1 >= ktop

    def unsettled(cge):
        return jnp.max(jnp.where(enough_keys & (cge != ktop), 1.0, 0.0))

    def lower_cond(carry):
        return (carry[0] < 32) & (carry[4] > 0.0)

    def lower_body(carry):
        it, tb, cge, crej, _ = carry
        for _ in range(LOWER_STEPS):
            cand_b = tb | jnp.left_shift(jnp.int32(1), 31 - it)
            cnt = above + count_upper((cand_b & 0xFFFF) + half_lo)
            tb, cge, crej = step(tb, cge, crej, cand_b, cnt)
            it = it + 1
        return it, tb, cge, crej, unsettled(cge)

    _, tb, cge, crej, _ = lax.while_loop(lower_cond, lower_body,
                                         (jnp.int32(16), tb, cge, crej, unsettled(cge)))
    thr = tb ^ jnp.int32(INT_MIN)
    has_thr = tb != 0

    need = ktop - crej
    any_tie = jnp.max(jnp.where(has_thr & (cge > ktop), 1.0, 0.0)) > 0.0

    @pl.when(any_tie)
    def _demote_surplus_ties():
        def body(it, p):
            cand = p | jnp.left_shift(jnp.int32(1), idx_bits - 1 - it)
            cnt = count(lambda blk, idx: (blk == thr) & (idx < cand))
            return jnp.where(cnt < need, cand, p)
        tie_last = lax.fori_loop(0, idx_bits, body, zeros_q)
        blk_iota = lax.broadcasted_iota(jnp.int32, (COUNT_ROWS, TQ), 0)

        def demote(r, carry):
            r0 = pl.multiple_of(r * COUNT_ROWS, COUNT_ROWS)
            blk = sc_ref[pl.ds(r0, COUNT_ROWS), :]
            surplus = has_thr & (blk == thr) & ((r0 + blk_iota) > tie_last)
            sc_ref[pl.ds(r0, COUNT_ROWS), :] = jnp.where(surplus, thr - 1, blk)
            return carry
        lax.fori_loop(0, n_rows // COUNT_ROWS, demote, 0)

    thr_ge = jnp.where(has_thr, thr, jnp.int32(INT_MIN + 1))

    pair_rows = lax.broadcasted_iota(jnp.int32, (2 * HEAD_DIM, TQ), 0)
    for h in range(N_HEADS):
        pair = qT_ref[(h // 2) * 2 * HEAD_DIM:(h // 2 + 1) * 2 * HEAD_DIM, :].astype(_F32)
        mine = (pair_rows >= HEAD_DIM) if h % 2 else (pair_rows < HEAD_DIM)
        qpad_ref[h] = jnp.where(mine, pair, 0.0).astype(qpad_ref.dtype)
    m_ref[...] = jnp.full(m_ref.shape, NEG_BIG, _F32)
    acc_ref[...] = jnp.zeros(acc_ref.shape, _F32)
    alpha_ref[...] = jnp.ones(alpha_ref.shape, _F32)
    for h in range(N_HEADS):
        p_refs[h][...] = jnp.zeros(p_refs[h].shape, p_refs[h].dtype)

    @pl.when((i + 1) % (TKF // TK) != 0)
    def _pad_keys():
        sc_ref[pl.ds(pl.multiple_of(n_rows, TK), TK), :] = jnp.full((TK, TQ), INT_MIN, jnp.int32)

    def apply_pv(kb):
        for h in range(N_HEADS):
            pv = jnp.dot(vT_ref[kb, h], p_refs[h][0], preferred_element_type=_F32)
            acc_ref[h] = alpha_ref[h:h + 1, :] * acc_ref[h] + pv

    def flash_body(kb, carry):
        apply_pv(jnp.maximum(kb - 1, 0))
        k0 = pl.multiple_of(kb * TKF, TKF)
        bias_ref[...] = jnp.where(sc_ref[pl.ds(k0, TKF), :] >= thr_ge, 0.0, NEG_BIG)
        for h in range(N_HEADS):
            for r in range(TKF // QK_ROWS):
                kpair = k_ref[pl.ds(k0 + r * QK_ROWS, QK_ROWS),
                              (h // 2) * 2 * HEAD_DIM:(h // 2 + 1) * 2 * HEAD_DIM]
                s_refs[h][0, r * QK_ROWS:(r + 1) * QK_ROWS, :] = jnp.dot(
                    kpair, qpad_ref[h], preferred_element_type=_F32)
        for h in range(N_HEADS):
            col_max = [None] * 4
            for j in range(TKF // SUBLANES):
                rows = slice(j * SUBLANES, (j + 1) * SUBLANES)
                slab = s_refs[h][0, rows, :] + bias_ref[rows, :]
                c = j % len(col_max)
                col_max[c] = slab if col_max[c] is None else jnp.maximum(col_max[c], slab)
            tile_max = jnp.maximum(jnp.maximum(col_max[0], col_max[1]),
                                   jnp.maximum(col_max[2], col_max[3]))
            m_old = m_ref[h:h + 1, :]
            m_new = jnp.maximum(m_old, jnp.max(tile_max, axis=0, keepdims=True))
            alpha_ref[h:h + 1, :] = jnp.exp2(m_old - m_new)
            m_ref[h:h + 1, :] = m_new
        for h in range(N_HEADS):
            p_refs[h][0] = jnp.exp2(s_refs[h][0] + bias_ref[...]
                                    - m_ref[h:h + 1, :]).astype(p_refs[h].dtype)
        return carry

    n_flash = (i + TKF // TK) // (TKF // TK)
    lax.fori_loop(0, n_flash, flash_body, 0)
    apply_pv(n_flash - 1)

    for h in range(N_HEADS):
        on_ref[h * HEAD_DIM:(h + 1) * HEAD_DIM, :] = (
            acc_ref[h, 0:HEAD_DIM, :] / acc_ref[h, HEAD_DIM:HEAD_DIM + 1, :])
    o_ref[...] = on_ref[...].T.astype(o_ref.dtype)


def _attention(qT, iqT, vT, wT, k, ik, ktop):
    B, attn_w, S = qT.shape
    idx_w = iqT.shape[1]
    idx_bits = max(1, math.ceil(math.log2(S)))
    grid = (B, S // TQ)
    return pl.pallas_call(
        functools.partial(_attn_kernel, ktop=ktop, idx_bits=idx_bits),
        grid=grid,
        in_specs=[
            pl.BlockSpec((None, idx_w, TQ), lambda b, i: (b, 0, i)),
            pl.BlockSpec((None, IDX_HEADS, TQ), lambda b, i: (b, 0, i)),
            pl.BlockSpec((None, attn_w, TQ), lambda b, i: (b, 0, i)),
            _resident((None, S, IDX_DIM), lambda b, i: (b, 0, 0)),
            _resident((None, S, attn_w), lambda b, i: (b, 0, 0)),
            _resident((None, S // TKF, N_HEADS, V_ROWS, TKF), lambda b, i: (b, 0, 0, 0, 0)),
        ],
        out_specs=pl.BlockSpec((None, TQ, attn_w), lambda b, i: (b, i, 0)),
        out_shape=jax.ShapeDtypeStruct((B, S, attn_w), MXU_DTYPE),
        scratch_shapes=[
            pltpu.VMEM((S, TQ), jnp.int32),
            pltpu.VMEM((S, TQ), jnp.int16),
            pltpu.VMEM((N_HEADS, 2 * HEAD_DIM, TQ), MXU_DTYPE),
            pltpu.VMEM((N_HEADS, V_ROWS, TQ), _F32),
            pltpu.VMEM((N_HEADS, TQ), _F32),
            pltpu.VMEM((N_HEADS, TQ), _F32),
            pltpu.VMEM((TKF, TQ), _F32),
            pltpu.VMEM((attn_w, TQ), _F32),
        ] + [pltpu.VMEM((1, TKF, TQ), _F32)] * N_HEADS
          + [pltpu.VMEM((1, TKF, TQ), MXU_DTYPE)] * N_HEADS,
        compiler_params=_params(),
        name="topk_attention",
    )(iqT, wT, qT, ik, k, vT)


def _mixer_kernel(x_ref, halo_ref, o_ref, g_ref, wpool_ref, wgate_ref, poolw_ref, pscale_ref,
                  wpp_ref, wap_ref, wout_ref, h_ref, u_ref, *, tm):
    i = pl.program_id(1)
    D = x_ref.shape[-1]
    group_dim = wpool_ref.shape[1] // POOL_GROUPS
    g = g_ref[...]
    x = x_ref[...]
    xn = _rmsnorm(x, g).astype(MXU_DTYPE)
    hn = _rmsnorm(halo_ref[...], g).astype(MXU_DTYPE)
    u_halo = jnp.dot(hn, wpool_ref[...], preferred_element_type=_F32)
    u_ref[0:POOL_HALO, :] = jnp.where(i > 0, u_halo, 0.0)
    u_ref[POOL_HALO:POOL_HALO + tm, :] = jnp.dot(xn, wpool_ref[...], preferred_element_type=_F32)

    t = i * tm + lax.broadcasted_iota(jnp.int32, (tm, group_dim), 0)
    mixed = []
    for gi, w in enumerate(POOL_WINDOWS):
        lanes = slice(gi * group_dim, (gi + 1) * group_dim)
        cur = u_ref[POOL_HALO:POOL_HALO + tm, lanes]
        sums = cur
        for j in range(1, w):
            sums = sums + u_ref[POOL_HALO - j:POOL_HALO - j + tm, lanes]
        cnt = jnp.minimum(t + 1, w).astype(_F32)
        pooled = sums / cnt - cur
        mixed.append(jnp.dot(pooled.astype(MXU_DTYPE), poolw_ref[gi], preferred_element_type=_F32))
    mixed = jnp.concatenate(mixed, axis=-1) * pscale_ref[...]
    y_pool = jnp.dot(mixed.astype(MXU_DTYPE), wpp_ref[...], preferred_element_type=_F32)
    y_attn = jnp.dot(o_ref[...], wap_ref[...], preferred_element_type=_F32)
    gates = jax.nn.sigmoid(jnp.dot(xn, wgate_ref[...], preferred_element_type=_F32))
    merged = gates[:, :D] * y_pool + gates[:, D:] * y_attn
    h_ref[...] = x + jnp.dot(merged.astype(MXU_DTYPE), wout_ref[...], preferred_element_type=_F32)


def _mixer_out(x, o, g, wpool, wgate, poolw, pscale, wpp, wap, wout, tm):
    B, S, D = x.shape
    pool_w = wpool.shape[1]
    attn_w = o.shape[-1]
    grid = (B, S // tm)
    const2 = lambda b, i: (0, 0)
    halo_blocks = tm // POOL_HALO
    return pl.pallas_call(
        functools.partial(_mixer_kernel, tm=tm),
        grid=grid,
        in_specs=[
            pl.BlockSpec((None, tm, D), lambda b, i: (b, i, 0)),
            pl.BlockSpec((None, POOL_HALO, D),
                         lambda b, i: (b, jnp.maximum(i * halo_blocks - 1, 0), 0)),
            pl.BlockSpec((None, tm, attn_w), lambda b, i: (b, i, 0)),
            pl.BlockSpec((1, D), const2),
            pl.BlockSpec(wpool.shape, const2),
            pl.BlockSpec(wgate.shape, const2),
            pl.BlockSpec(poolw.shape, lambda b, i: (0, 0, 0)),
            pl.BlockSpec((1, pool_w), const2),
            pl.BlockSpec(wpp.shape, const2),
            pl.BlockSpec(wap.shape, const2),
            pl.BlockSpec(wout.shape, const2),
        ],
        out_specs=pl.BlockSpec((None, tm, D), lambda b, i: (b, i, 0)),
        out_shape=jax.ShapeDtypeStruct((B, S, D), _F32),
        scratch_shapes=[pltpu.VMEM((POOL_HALO + tm, pool_w), _F32)],
        compiler_params=_params(),
        name="mixer_out",
    )(x, x, o, g, wpool, wgate, poolw, pscale, wpp, wap, wout)


def _ffn_kernel(h_ref, halo_ref, g_ref, wa_ref, wb_ref, cwa_ref, cwb_ref, cba_ref, cbb_ref,
                wd_ref, gf_ref, out_ref, hn_ref, acc_ref, ua0_ref, ua1_ref, ub0_ref, ub1_ref,
                *, tm, n_chunks):
    ua_refs, ub_refs = (ua0_ref, ua1_ref), (ub0_ref, ub1_ref)
    i = pl.program_id(1)
    g = g_ref[...]
    h = h_ref[...]
    hn_ref[0:FFN_HALO, :] = _rmsnorm(halo_ref[...], g).astype(MXU_DTYPE)
    hn_ref[FFN_HALO:FFN_HALO + tm, :] = _rmsnorm(h, g).astype(MXU_DTYPE)
    acc_ref[...] = jnp.zeros(acc_ref.shape, _F32)
    first = i == 0

    def conv(u_ref, up, cw, cb):
        u_ref[FFN_HALO:FFN_HALO + tm, :] = up[FFN_HALO:, :]
        u_ref[0:FFN_HALO, :] = jnp.where(first, 0.0, up[0:FFN_HALO, :])
        out = cb
        for j in range(CONV_WIDTH):
            off = FFN_HALO - (CONV_WIDTH - 1) + j
            out = out + u_ref[off:off + tm, :] * cw[j:j + 1, :]
        return out

    for c in range(n_chunks):
        hn = hn_ref[...]
        a = conv(ua_refs[c % 2], jnp.dot(hn, wa_ref[c], preferred_element_type=_F32),
                 cwa_ref[c], cba_ref[c])
        b = conv(ub_refs[c % 2], jnp.dot(hn, wb_ref[c], preferred_element_type=_F32),
                 cwb_ref[c], cbb_ref[c])
        act = (a * jax.nn.sigmoid(a) * b).astype(MXU_DTYPE)
        acc_ref[...] += jnp.dot(act, wd_ref[c], preferred_element_type=_F32)
    out_ref[...] = _rmsnorm(h + acc_ref[...], gf_ref[...])


def _conv_ffn(h, g, wa, wb, cwa, cwb, cba, cbb, wd, gf, tm):
    B, S, D = h.shape
    n_chunks = wa.shape[0]
    grid = (B, S // tm)
    const2 = lambda b, i: (0, 0)
    const3 = lambda b, i: (0, 0, 0)
    halo_blocks = tm // FFN_HALO
    return pl.pallas_call(
        functools.partial(_ffn_kernel, tm=tm, n_chunks=n_chunks),
        grid=grid,
        in_specs=[
            pl.BlockSpec((None, tm, D), lambda b, i: (b, i, 0)),
            pl.BlockSpec((None, FFN_HALO, D),
                         lambda b, i: (b, jnp.maximum(i * halo_blocks - 1, 0), 0)),
            pl.BlockSpec((1, D), const2),
            _resident(wa.shape, const3),
            _resident(wb.shape, const3),
            pl.BlockSpec(cwa.shape, const3),
            pl.BlockSpec(cwb.shape, const3),
            pl.BlockSpec(cba.shape, const3),
            pl.BlockSpec(cbb.shape, const3),
            _resident(wd.shape, const3),
            pl.BlockSpec((1, D), const2),
        ],
        out_specs=pl.BlockSpec((None, tm, D), lambda b, i: (b, i, 0)),
        out_shape=jax.ShapeDtypeStruct((B, S, D), _F32),
        scratch_shapes=[
            pltpu.VMEM((FFN_HALO + tm, D), MXU_DTYPE),
            pltpu.VMEM((tm, D), _F32),
        ] + [pltpu.VMEM((FFN_HALO + tm, FF_CHUNK), _F32)] * 4,
        compiler_params=_params(),
        name="conv_ffn",
    )(h, h, g, wa, wb, cwa, cwb, cba, cbb, wd, gf)


def _rope_tables(S):
    half = ROT_DIM // 2
    inv_freq = 1.0 / (ROPE_THETA ** (jnp.arange(half, dtype=_F32) * 2.0 / ROT_DIM))
    ang = jnp.arange(S, dtype=_F32)[:, None] * inv_freq[None, :]
    cos, sin = jnp.cos(ang), jnp.sin(ang)
    zeros = jnp.zeros((S, HEAD_DIM - ROT_DIM), _F32)
    zh = jnp.zeros((S, half), _F32)
    ra = jnp.concatenate([cos, cos, jnp.ones_like(zeros)], axis=1)
    rb = jnp.concatenate([zh, sin, zeros], axis=1)
    rc = jnp.concatenate([-sin, zh, zeros], axis=1)
    rep = LANES // HEAD_DIM
    tile = lambda a: jnp.tile(a, (1, rep))
    return cos.T, sin.T, tile(ra), tile(rb), tile(rc)


def _layer(h, norm_mix_g, w_in, pool_w, pool_scale, w_pool_proj, w_attn_proj, w_out,
           norm_ffn_g, w_up, conv_w, conv_b, w_down, norm_out_g, tables):
    B, S, D = h.shape
    pool_width = D // 2
    attn_w = N_HEADS * HEAD_DIM
    idx_w = IDX_HEADS * IDX_DIM
    d_ff = w_down.shape[0]
    ktop = min(TOPK_MAX, S // 4)
    c0 = pool_width
    cq, ck, cv = c0, c0 + attn_w, c0 + 2 * attn_w
    ciq = c0 + 3 * attn_w
    cik = ciq + idx_w
    ciw = cik + IDX_DIM
    cg = ciw + IDX_HEADS
    cast = lambda a: a.astype(MXU_DTYPE)

    wrow = cast(jnp.concatenate(
        [w_in[:, ck:cv], w_in[:, cik:ciw], jnp.zeros((D, LANES - IDX_DIM), w_in.dtype)], axis=1))
    wt = cast(jnp.concatenate(
        [w_in[:, cq:ck], w_in[:, ciq:cik], w_in[:, cv:ciq], w_in[:, ciw:cg],
         jnp.zeros((D, 16 - IDX_HEADS), w_in.dtype)], axis=1).T)
    row1 = lambda a: a.reshape(1, -1)

    tm = TKF
    qT, iqT, vT, wT, k, ik = _projection(h, row1(norm_mix_g), wrow, wt, *tables, tm=tm)
    o = _attention(qT, iqT, vT, wT, k, ik, ktop)
    h1 = _mixer_out(h, o, row1(norm_mix_g), cast(w_in[:, :c0]), cast(w_in[:, cg:]),
                    cast(pool_w), row1(pool_scale), cast(w_pool_proj), cast(w_attn_proj),
                    cast(w_out), tm=tm)

    n_chunks = d_ff // FF_CHUNK
    split_cols = lambda a: a.reshape(a.shape[0], n_chunks, FF_CHUNK).transpose(1, 0, 2)
    wa, wb = cast(split_cols(w_up[:, :d_ff])), cast(split_cols(w_up[:, d_ff:]))
    cwa, cwb = split_cols(conv_w[:, :d_ff]), split_cols(conv_w[:, d_ff:])
    cba, cbb = split_cols(conv_b[None, :d_ff]), split_cols(conv_b[None, d_ff:])
    wd = cast(w_down.reshape(n_chunks, FF_CHUNK, D))
    return _conv_ffn(h1, row1(norm_ffn_g), wa, wb, cwa, cwb, cba, cbb, wd, row1(norm_out_g), tm=tm)


def kernel(x, norm_mix_g, w_in, pool_w, pool_scale, w_pool_proj, w_attn_proj, w_out, norm_ffn_g,
           w_up, conv_w, conv_b, w_down, norm_final_g):
    depth = w_in.shape[0]
    assert depth == 1, "the final RMSNorm is fused into the last layer's ConvFFN kernel"
    S = x.shape[1]
    assert S % TKF == 0 and TQ == TK and TKF % TK == 0
    tables = _rope_tables(S)
    return _layer(x, norm_mix_g[0], w_in[0], pool_w[0], pool_scale[0], w_pool_proj[0],
                  w_attn_proj[0], w_out[0], norm_ffn_g[0], w_up[0], conv_w[0], conv_b[0],
                  w_down[0], norm_final_g, tables)
```

```python
import functools
import math

import jax
import jax.numpy as jnp
from jax import lax
from jax.experimental import pallas as pl
from jax.experimental.pallas import tpu as pltpu

POOL_GROUPS = 4
POOL_WINDOWS = (2, 4, 8, 16)
N_HEADS = 8
HEAD_DIM = 64
ROT_DIM = HEAD_DIM // 4
ROPE_THETA = 500000.0
IDX_HEADS = 8
IDX_DIM = 64
TOPK_MAX = 256
N_BRANCH = 2
CONV_WIDTH = 3
EPS = 1e-6

LANES = 128
SUBLANES = 8
PACK_ROWS = 16
VMEM_LIMIT_BYTES = 56 * 1024 * 1024
MXU_DTYPE = jnp.bfloat16

TQ = 256
TK = 256
TKF = 512
QK_ROWS = 256
SCORE_ROWS = 64
COUNT_ROWS = 256
LOWER_STEPS = 4
V_ROWS = HEAD_DIM + 16
LOG2E = 1.4426950408889634
POOL_HALO = 16
FFN_HALO = 16
FF_CHUNK = 256

INT_MIN = -(2 ** 31)
NEG_BIG = -1e30

_F32 = jnp.float32


def _rmsnorm(x, g):
    return x * lax.rsqrt(jnp.mean(x * x, axis=-1, keepdims=True) + EPS) * g


def _params(**kw):
    return pltpu.CompilerParams(vmem_limit_bytes=VMEM_LIMIT_BYTES, **kw)


def _resident(block_shape, index_map):
    return pl.BlockSpec(block_shape, index_map, pipeline_mode=pl.Buffered(1))


def _proj_kernel(x_ref, g_ref, wrow_ref, wt_ref, cos_ref, sin_ref, ra_ref, rb_ref, rc_ref,
                 qT_ref, iqT_ref, vT_ref, wT_ref, k_ref, ik_ref, *, n_vt):
    attn_w = N_HEADS * HEAD_DIM
    idx_w = IDX_HEADS * IDX_DIM
    xn = _rmsnorm(x_ref[...], g_ref[...]).astype(MXU_DTYPE)
    row = jnp.dot(xn, wrow_ref[...], preferred_element_type=_F32)
    tr = lax.dot_general(wt_ref[...], xn, (((1,), (1,)), ((), ())),
                         preferred_element_type=_F32)

    cos = cos_ref[...]
    sin = sin_ref[...]
    half = ROT_DIM // 2

    def rope_t(z, n_heads, dim):
        parts = []
        for h in range(n_heads):
            b = h * dim
            x1 = z[b:b + half]
            x2 = z[b + half:b + ROT_DIM]
            parts += [x1 * cos - x2 * sin, x2 * cos + x1 * sin, z[b + ROT_DIM:b + dim]]
        return jnp.concatenate(parts, axis=0)

    qT_ref[...] = (rope_t(tr[0:attn_w], N_HEADS, HEAD_DIM)
                   * (HEAD_DIM ** -0.5 * LOG2E)).astype(qT_ref.dtype)
    iqT_ref[...] = rope_t(tr[attn_w:attn_w + idx_w], IDX_HEADS, IDX_DIM).astype(iqT_ref.dtype)
    vt = tr[attn_w + idx_w:2 * attn_w + idx_w]
    ones_rows = (lax.broadcasted_iota(jnp.int32, (V_ROWS - HEAD_DIM, TKF), 0) == 0).astype(_F32)
    for j in range(n_vt):
        for h in range(N_HEADS):
            vT_ref[j, h, 0:HEAD_DIM, :] = vt[h * HEAD_DIM:(h + 1) * HEAD_DIM,
                                             j * TKF:(j + 1) * TKF].astype(vT_ref.dtype)
            vT_ref[j, h, HEAD_DIM:V_ROWS, :] = ones_rows.astype(vT_ref.dtype)
    wT_ref[...] = tr[2 * attn_w + idx_w:2 * attn_w + idx_w + IDX_HEADS] * (
        (IDX_HEADS ** -0.5) * (IDX_DIM ** -0.5))

    ra = ra_ref[...]
    rb = rb_ref[...]
    rc = rc_ref[...]

    def rope_rows(c):
        return (c * ra + pltpu.roll(c, half, 1) * rb + pltpu.roll(c, LANES - half, 1) * rc)

    for c in range(attn_w // LANES):
        k_ref[:, c * LANES:(c + 1) * LANES] = rope_rows(
            row[:, c * LANES:(c + 1) * LANES]).astype(k_ref.dtype)
    ik_ref[...] = rope_rows(row[:, attn_w:attn_w + LANES])[:, :IDX_DIM].astype(ik_ref.dtype)


def _projection(x, g, wrow, wt, cosT, sinT, ra, rb, rc, tm):
    B, S, D = x.shape
    attn_w = N_HEADS * HEAD_DIM
    idx_w = IDX_HEADS * IDX_DIM
    n_vt = tm // TKF
    grid = (B, S // tm)
    const = lambda b, i: (0, 0)
    out_shape = (
        jax.ShapeDtypeStruct((B, attn_w, S), MXU_DTYPE),
        jax.ShapeDtypeStruct((B, idx_w, S), MXU_DTYPE),
        jax.ShapeDtypeStruct((B, S // TKF, N_HEADS, V_ROWS, TKF), MXU_DTYPE),
        jax.ShapeDtypeStruct((B, IDX_HEADS, S), _F32),
        jax.ShapeDtypeStruct((B, S, attn_w), MXU_DTYPE),
        jax.ShapeDtypeStruct((B, S, IDX_DIM), MXU_DTYPE),
    )
    return pl.pallas_call(
        functools.partial(_proj_kernel, n_vt=n_vt),
        grid=grid,
        in_specs=[
            pl.BlockSpec((None, tm, D), lambda b, i: (b, i, 0)),
            pl.BlockSpec((1, D), const),
            pl.BlockSpec(wrow.shape, const),
            pl.BlockSpec(wt.shape, const),
            pl.BlockSpec((ROT_DIM // 2, tm), lambda b, i: (0, i)),
            pl.BlockSpec((ROT_DIM // 2, tm), lambda b, i: (0, i)),
            pl.BlockSpec((tm, LANES), lambda b, i: (i, 0)),
            pl.BlockSpec((tm, LANES), lambda b, i: (i, 0)),
            pl.BlockSpec((tm, LANES), lambda b, i: (i, 0)),
        ],
        out_specs=(
            pl.BlockSpec((None, attn_w, tm), lambda b, i: (b, 0, i)),
            pl.BlockSpec((None, idx_w, tm), lambda b, i: (b, 0, i)),
            pl.BlockSpec((None, n_vt, N_HEADS, V_ROWS, TKF), lambda b, i: (b, i, 0, 0, 0)),
            pl.BlockSpec((None, IDX_HEADS, tm), lambda b, i: (b, 0, i)),
            pl.BlockSpec((None, tm, attn_w), lambda b, i: (b, i, 0)),
            pl.BlockSpec((None, tm, IDX_DIM), lambda b, i: (b, i, 0)),
        ),
        out_shape=out_shape,
        compiler_params=_params(),
        name="proj_rope",
    )(x, g, wrow, wt, cosT, sinT, ra, rb, rc)


def _attn_kernel(iqT_ref, wT_ref, qT_ref, ik_ref, k_ref, vT_ref, o_ref,
                 sc_ref, hi_ref, qpad_ref, acc_ref, m_ref, alpha_ref, bias_ref, on_ref, *stage_refs,
                 ktop, idx_bits):
    s_refs, p_refs = stage_refs[:N_HEADS], stage_refs[N_HEADS:]
    i = pl.program_id(1)
    n_rows = (i + 1) * TK

    sub_rows = lax.broadcasted_iota(jnp.int32, (SCORE_ROWS, TQ), 0)
    sub_cols = lax.broadcasted_iota(jnp.int32, (SCORE_ROWS, TQ), 1)

    def score_tile(kb, diagonal):
        k0 = pl.multiple_of(kb * TK, TK)
        for sub in range(TK // SCORE_ROWS):
            r0 = k0 + sub * SCORE_ROWS
            ikb = ik_ref[pl.ds(r0, SCORE_ROWS), :]
            score = None
            for h in range(IDX_HEADS):
                s = jnp.dot(ikb, iqT_ref[h * IDX_DIM:(h + 1) * IDX_DIM, :],
                            preferred_element_type=_F32)
                term = jnp.maximum(s, 0.0) * wT_ref[h:h + 1, :]
                score = term if score is None else score + term
            score = jnp.where(score == 0.0, 0.0, score)
            bits = pltpu.bitcast(score, jnp.int32)
            key = jnp.where(bits < 0, bits ^ jnp.int32(0x7FFFFFFF), bits)
            if diagonal:
                causal = sub_rows + sub * SCORE_ROWS <= sub_cols
                key = jnp.where(causal, key, jnp.int32(INT_MIN))
            sc_ref[pl.ds(r0, SCORE_ROWS), :] = key
            hi_ref[pl.ds(r0, SCORE_ROWS), :] = jnp.right_shift(key, 16).astype(hi_ref.dtype)

    def score_body(kb, carry):
        score_tile(kb, False)
        return carry

    lax.fori_loop(0, i, score_body, 0)
    score_tile(i, True)

    n_pairs = (i + TKF // TK) // (TKF // TK)

    @pl.when((i + 1) % (TKF // TK) != 0)
    def _pad_keys():
        sc_ref[pl.ds(pl.multiple_of(n_rows, TK), TK), :] = jnp.full((TK, TQ), INT_MIN, jnp.int32)

    row8_iota = lax.broadcasted_iota(jnp.int32, (SUBLANES, TQ), 0)

    def count(pred):
        def body(r, cnt):
            r0 = pl.multiple_of(r * COUNT_ROWS, COUNT_ROWS)
            blk = sc_ref[pl.ds(r0, COUNT_ROWS), :]
            parts = []
            for j in range(COUNT_ROWS // SUBLANES):
                parts.append(pred(blk[j * SUBLANES:(j + 1) * SUBLANES],
                                  r0 + j * SUBLANES + row8_iota).astype(jnp.int32))
            while len(parts) > 1:
                parts = [parts[a] + parts[a + 1] for a in range(0, len(parts), 2)]
            return cnt + parts[0]
        cnt = lax.fori_loop(0, n_rows // COUNT_ROWS, body, jnp.zeros((SUBLANES, TQ), jnp.int32))
        return jnp.sum(cnt.astype(_F32), axis=0, keepdims=True)

    def step(tb, cge, crej, cand_b, cnt):
        ok = cnt >= ktop
        return (jnp.where(ok, cand_b, tb), jnp.where(ok, cnt, cge), jnp.where(ok, crej, cnt))

    def count_upper(cand_hi):
        cand16 = jnp.broadcast_to(cand_hi, (PACK_ROWS, TQ)).astype(hi_ref.dtype)
        one16 = jnp.ones((PACK_ROWS, TQ), hi_ref.dtype)
        zero16 = jnp.zeros((PACK_ROWS, TQ), hi_ref.dtype)

        def body(r, cnt):
            r0 = pl.multiple_of(r * COUNT_ROWS, COUNT_ROWS)
            blk = hi_ref[pl.ds(r0, COUNT_ROWS), :]
            parts = [jnp.where(blk[j * PACK_ROWS:(j + 1) * PACK_ROWS] >= cand16, one16, zero16)
                     for j in range(COUNT_ROWS // PACK_ROWS)]
            while len(parts) > 1:
                parts = [parts[a] + parts[a + 1] for a in range(0, len(parts), 2)]
            return cnt + parts[0].astype(jnp.int32)
        cnt = lax.fori_loop(0, n_rows // COUNT_ROWS, body, jnp.zeros((PACK_ROWS, TQ), jnp.int32))
        return jnp.sum(cnt.astype(_F32), axis=0, keepdims=True)

    def upper_body(it, carry):
        tb = carry[0]
        cand_b = tb | jnp.left_shift(jnp.int32(1), 31 - it)
        cand = cand_b ^ jnp.int32(INT_MIN)
        return step(*carry, cand_b, count_upper(jnp.right_shift(cand, 16)))

    zeros_q = jnp.zeros((1, TQ), jnp.int32)
    zeros_f = jnp.zeros((1, TQ), _F32)
    tb, cge, crej = lax.fori_loop(0, 16, upper_body, (zeros_q, zeros_f, zeros_f))

    above = crej
    thr_hi = jnp.right_shift(tb ^ jnp.int32(INT_MIN), 16)
    half_lo = -(1 << 15)

    def lower_plane(r, carry):
        r0 = pl.multiple_of(r * COUNT_ROWS, COUNT_ROWS)
        key = sc_ref[pl.ds(r0, COUNT_ROWS), :]
        low = jnp.where(jnp.right_shift(key, 16) == thr_hi, (key & 0xFFFF) + half_lo, half_lo)
        hi_ref[pl.ds(r0, COUNT_ROWS), :] = low.astype(hi_ref.dtype)
        return carry

    lax.fori_loop(0, n_rows // COUNT_ROWS, lower_plane, 0)

    lane = lax.broadcasted_iota(jnp.int32, (1, TQ), 1)
    enough_keys = i * TQ + lane + 1 >= ktop

    def unsettled(cge):
        return jnp.max(jnp.where(enough_keys & (cge != ktop), 1.0, 0.0))

    def lower_cond(carry):
        return (carry[0] < 32) & (carry[4] > 0.0)

    def lower_body(carry):
        it, tb, cge, crej, _ = carry
        for _ in range(LOWER_STEPS):
            cand_b = tb | jnp.left_shift(jnp.int32(1), 31 - it)
            cnt = above + count_upper((cand_b & 0xFFFF) + half_lo)
            tb, cge, crej = step(tb, cge, crej, cand_b, cnt)
            it = it + 1
        return it, tb, cge, crej, unsettled(cge)

    _, tb, cge, crej, _ = lax.while_loop(lower_cond, lower_body,
                                         (jnp.int32(16), tb, cge, crej, unsettled(cge)))
    thr = tb ^ jnp.int32(INT_MIN)
    has_thr = tb != 0

    need = ktop - crej
    any_tie = jnp.max(jnp.where(has_thr & (cge > ktop), 1.0, 0.0)) > 0.0

    @pl.when(any_tie)
    def _demote_surplus_ties():
        def body(it, p):
            cand = p | jnp.left_shift(jnp.int32(1), idx_bits - 1 - it)
            cnt = count(lambda blk, idx: (blk == thr) & (idx < cand))
            return jnp.where(cnt < need, cand, p)
        tie_last = lax.fori_loop(0, idx_bits, body, zeros_q)
        blk_iota = lax.broadcasted_iota(jnp.int32, (COUNT_ROWS, TQ), 0)

        def demote(r, carry):
            r0 = pl.multiple_of(r * COUNT_ROWS, COUNT_ROWS)
            blk = sc_ref[pl.ds(r0, COUNT_ROWS), :]
            surplus = has_thr & (blk == thr) & ((r0 + blk_iota) > tie_last)
            sc_ref[pl.ds(r0, COUNT_ROWS), :] = jnp.where(surplus, thr - 1, blk)
            return carry
        lax.fori_loop(0, n_rows // COUNT_ROWS, demote, 0)

    thr_ge = jnp.where(has_thr, thr, jnp.int32(INT_MIN + 1))

    pair_rows = lax.broadcasted_iota(jnp.int32, (2 * HEAD_DIM, TQ), 0)
    for h in range(N_HEADS):
        pair = qT_ref[(h // 2) * 2 * HEAD_DIM:(h // 2 + 1) * 2 * HEAD_DIM, :].astype(_F32)
        mine = (pair_rows >= HEAD_DIM) if h % 2 else (pair_rows < HEAD_DIM)
        qpad_ref[h] = jnp.where(mine, pair, 0.0).astype(qpad_ref.dtype)
    m_ref[...] = jnp.full(m_ref.shape, NEG_BIG, _F32)
    acc_ref[...] = jnp.zeros(acc_ref.shape, _F32)
    alpha_ref[...] = jnp.ones(alpha_ref.shape, _F32)
    for h in range(N_HEADS):
        p_refs[h][...] = jnp.zeros(p_refs[h].shape, p_refs[h].dtype)

    def apply_pv(kb):
        for h in range(N_HEADS):
            pv = jnp.dot(vT_ref[kb, h], p_refs[h][0], preferred_element_type=_F32)
            acc_ref[h] = alpha_ref[h:h + 1, :] * acc_ref[h] + pv

    def flash_body(kb, carry):
        apply_pv(jnp.maximum(kb - 1, 0))
        k0 = pl.multiple_of(kb * TKF, TKF)
        bias_ref[...] = jnp.where(sc_ref[pl.ds(k0, TKF), :] >= thr_ge, 0.0, NEG_BIG)
        for h in range(N_HEADS):
            for r in range(TKF // QK_ROWS):
                kpair = k_ref[pl.ds(k0 + r * QK_ROWS, QK_ROWS),
                              (h // 2) * 2 * HEAD_DIM:(h // 2 + 1) * 2 * HEAD_DIM]
                s_refs[h][0, r * QK_ROWS:(r + 1) * QK_ROWS, :] = jnp.dot(
                    kpair, qpad_ref[h], preferred_element_type=_F32)
        for h in range(N_HEADS):
            col_max = [None] * 4
            for j in range(TKF // SUBLANES):
                rows = slice(j * SUBLANES, (j + 1) * SUBLANES)
                slab = s_refs[h][0, rows, :] + bias_ref[rows, :]
                c = j % len(col_max)
                col_max[c] = slab if col_max[c] is None else jnp.maximum(col_max[c], slab)
            tile_max = jnp.maximum(jnp.maximum(col_max[0], col_max[1]),
                                   jnp.maximum(col_max[2], col_max[3]))
            m_old = m_ref[h:h + 1, :]
            m_new = jnp.maximum(m_old, jnp.max(tile_max, axis=0, keepdims=True))
            alpha_ref[h:h + 1, :] = jnp.exp2(m_old - m_new)
            m_ref[h:h + 1, :] = m_new
        for h in range(N_HEADS):
            p_refs[h][0] = jnp.exp2(s_refs[h][0] + bias_ref[...]
                                    - m_ref[h:h + 1, :]).astype(p_refs[h].dtype)
        return carry

    lax.fori_loop(0, n_pairs, flash_body, 0)
    apply_pv(n_pairs - 1)

    for h in range(N_HEADS):
        on_ref[h * HEAD_DIM:(h + 1) * HEAD_DIM, :] = (
            acc_ref[h, 0:HEAD_DIM, :] / acc_ref[h, HEAD_DIM:HEAD_DIM + 1, :])
    o_ref[...] = on_ref[...].T.astype(o_ref.dtype)


def _attention(qT, iqT, vT, wT, k, ik, ktop):
    B, attn_w, S = qT.shape
    idx_w = iqT.shape[1]
    idx_bits = max(1, math.ceil(math.log2(S)))
    grid = (B, S // TQ)
    return pl.pallas_call(
        functools.partial(_attn_kernel, ktop=ktop, idx_bits=idx_bits),
        grid=grid,
        in_specs=[
            pl.BlockSpec((None, idx_w, TQ), lambda b, i: (b, 0, i)),
            pl.BlockSpec((None, IDX_HEADS, TQ), lambda b, i: (b, 0, i)),
            pl.BlockSpec((None, attn_w, TQ), lambda b, i: (b, 0, i)),
            _resident((None, S, IDX_DIM), lambda b, i: (b, 0, 0)),
            _resident((None, S, attn_w), lambda b, i: (b, 0, 0)),
            _resident((None, S // TKF, N_HEADS, V_ROWS, TKF), lambda b, i: (b, 0, 0, 0, 0)),
        ],
        out_specs=pl.BlockSpec((None, TQ, attn_w), lambda b, i: (b, i, 0)),
        out_shape=jax.ShapeDtypeStruct((B, S, attn_w), MXU_DTYPE),
        scratch_shapes=[
            pltpu.VMEM((S, TQ), jnp.int32),
            pltpu.VMEM((S, TQ), jnp.int16),
            pltpu.VMEM((N_HEADS, 2 * HEAD_DIM, TQ), MXU_DTYPE),
            pltpu.VMEM((N_HEADS, V_ROWS, TQ), _F32),
            pltpu.VMEM((N_HEADS, TQ), _F32),
            pltpu.VMEM((N_HEADS, TQ), _F32),
            pltpu.VMEM((TKF, TQ), _F32),
            pltpu.VMEM((attn_w, TQ), _F32),
        ] + [pltpu.VMEM((1, TKF, TQ), _F32)] * N_HEADS
          + [pltpu.VMEM((1, TKF, TQ), MXU_DTYPE)] * N_HEADS,
        compiler_params=_params(),
        name="topk_attention",
    )(iqT, wT, qT, ik, k, vT)


def _mixer_kernel(x_ref, halo_ref, o_ref, g_ref, wpool_ref, wgate_ref, poolw_ref, pscale_ref,
                  wpp_ref, wap_ref, wout_ref, h_ref, u_ref, *, tm):
    i = pl.program_id(1)
    D = x_ref.shape[-1]
    group_dim = wpool_ref.shape[1] // POOL_GROUPS
    g = g_ref[...]
    x = x_ref[...]
    xn = _rmsnorm(x, g).astype(MXU_DTYPE)
    hn = _rmsnorm(halo_ref[...], g).astype(MXU_DTYPE)
    u_halo = jnp.dot(hn, wpool_ref[...], preferred_element_type=_F32)
    u_ref[0:POOL_HALO, :] = jnp.where(i > 0, u_halo, 0.0)
    u_ref[POOL_HALO:POOL_HALO + tm, :] = jnp.dot(xn, wpool_ref[...], preferred_element_type=_F32)

    t = i * tm + lax.broadcasted_iota(jnp.int32, (tm, group_dim), 0)
    mixed = []
    for gi, w in enumerate(POOL_WINDOWS):
        lanes = slice(gi * group_dim, (gi + 1) * group_dim)
        cur = u_ref[POOL_HALO:POOL_HALO + tm, lanes]
        sums = cur
        for j in range(1, w):
            sums = sums + u_ref[POOL_HALO - j:POOL_HALO - j + tm, lanes]
        cnt = jnp.minimum(t + 1, w).astype(_F32)
        pooled = sums / cnt - cur
        mixed.append(jnp.dot(pooled.astype(MXU_DTYPE), poolw_ref[gi], preferred_element_type=_F32))
    mixed = jnp.concatenate(mixed, axis=-1) * pscale_ref[...]
    y_pool = jnp.dot(mixed.astype(MXU_DTYPE), wpp_ref[...], preferred_element_type=_F32)
    y_attn = jnp.dot(o_ref[...], wap_ref[...], preferred_element_type=_F32)
    gates = jax.nn.sigmoid(jnp.dot(xn, wgate_ref[...], preferred_element_type=_F32))
    merged = gates[:, :D] * y_pool + gates[:, D:] * y_attn
    h_ref[...] = x + jnp.dot(merged.astype(MXU_DTYPE), wout_ref[...], preferred_element_type=_F32)


def _mixer_out(x, o, g, wpool, wgate, poolw, pscale, wpp, wap, wout, tm):
    B, S, D = x.shape
    pool_w = wpool.shape[1]
    attn_w = o.shape[-1]
    grid = (B, S // tm)
    const2 = lambda b, i: (0, 0)
    halo_blocks = tm // POOL_HALO
    return pl.pallas_call(
        functools.partial(_mixer_kernel, tm=tm),
        grid=grid,
        in_specs=[
            pl.BlockSpec((None, tm, D), lambda b, i: (b, i, 0)),
            pl.BlockSpec((None, POOL_HALO, D),
                         lambda b, i: (b, jnp.maximum(i * halo_blocks - 1, 0), 0)),
            pl.BlockSpec((None, tm, attn_w), lambda b, i: (b, i, 0)),
            pl.BlockSpec((1, D), const2),
            pl.BlockSpec(wpool.shape, const2),
            pl.BlockSpec(wgate.shape, const2),
            pl.BlockSpec(poolw.shape, lambda b, i: (0, 0, 0)),
            pl.BlockSpec((1, pool_w), const2),
            pl.BlockSpec(wpp.shape, const2),
            pl.BlockSpec(wap.shape, const2),
            pl.BlockSpec(wout.shape, const2),
        ],
        out_specs=pl.BlockSpec((None, tm, D), lambda b, i: (b, i, 0)),
        out_shape=jax.ShapeDtypeStruct((B, S, D), _F32),
        scratch_shapes=[pltpu.VMEM((POOL_HALO + tm, pool_w), _F32)],
        compiler_params=_params(),
        name="mixer_out",
    )(x, x, o, g, wpool, wgate, poolw, pscale, wpp, wap, wout)


def _ffn_kernel(h_ref, halo_ref, g_ref, wup_ref, cw_ref, cb_ref, wd_ref, gf_ref, out_ref,
                hn_ref, act_ref, ua0_ref, ua1_ref, ub0_ref, ub1_ref, *, tm, d_ff):
    ua_refs, ub_refs = (ua0_ref, ua1_ref), (ub0_ref, ub1_ref)
    i = pl.program_id(1)
    g = g_ref[...]
    h = h_ref[...]
    hn_ref[0:FFN_HALO, :] = _rmsnorm(halo_ref[...], g).astype(MXU_DTYPE)
    hn_ref[FFN_HALO:FFN_HALO + tm, :] = _rmsnorm(h, g).astype(MXU_DTYPE)
    first = i == 0

    def conv(u_ref, up, cw, cb):
        u_ref[FFN_HALO:FFN_HALO + tm, :] = up[FFN_HALO:, :]
        u_ref[0:FFN_HALO, :] = jnp.where(first, 0.0, up[0:FFN_HALO, :])
        out = cb
        for j in range(CONV_WIDTH):
            off = FFN_HALO - (CONV_WIDTH - 1) + j
            out = out + u_ref[off:off + tm, :] * cw[j:j + 1, :]
        return out

    for c in range(d_ff // FF_CHUNK):
        hn = hn_ref[...]
        cols_a = slice(c * FF_CHUNK, (c + 1) * FF_CHUNK)
        cols_b = slice(d_ff + c * FF_CHUNK, d_ff + (c + 1) * FF_CHUNK)
        a = conv(ua_refs[c % 2], jnp.dot(hn, wup_ref[:, cols_a], preferred_element_type=_F32),
                 cw_ref[:, cols_a], cb_ref[:, cols_a])
        b = conv(ub_refs[c % 2], jnp.dot(hn, wup_ref[:, cols_b], preferred_element_type=_F32),
                 cw_ref[:, cols_b], cb_ref[:, cols_b])
        act_ref[:, cols_a] = (a * jax.nn.sigmoid(a) * b).astype(MXU_DTYPE)
    down = jnp.dot(act_ref[...], wd_ref[...], preferred_element_type=_F32)
    out_ref[...] = _rmsnorm(h + down, gf_ref[...])


def _conv_ffn(h, g, wup, cw, cb, wd, gf, tm):
    B, S, D = h.shape
    d_ff = wd.shape[0]
    grid = (B, S // tm)
    const2 = lambda b, i: (0, 0)
    halo_blocks = tm // FFN_HALO
    return pl.pallas_call(
        functools.partial(_ffn_kernel, tm=tm, d_ff=d_ff),
        grid=grid,
        in_specs=[
            pl.BlockSpec((None, tm, D), lambda b, i: (b, i, 0)),
            pl.BlockSpec((None, FFN_HALO, D),
                         lambda b, i: (b, jnp.maximum(i * halo_blocks - 1, 0), 0)),
            pl.BlockSpec((1, D), const2),
            _resident(wup.shape, const2),
            pl.BlockSpec(cw.shape, const2),
            pl.BlockSpec(cb.shape, const2),
            _resident(wd.shape, const2),
            pl.BlockSpec((1, D), const2),
        ],
        out_specs=pl.BlockSpec((None, tm, D), lambda b, i: (b, i, 0)),
        out_shape=jax.ShapeDtypeStruct((B, S, D), _F32),
        scratch_shapes=[
            pltpu.VMEM((FFN_HALO + tm, D), MXU_DTYPE),
            pltpu.VMEM((tm, d_ff), MXU_DTYPE),
        ] + [pltpu.VMEM((FFN_HALO + tm, FF_CHUNK), _F32)] * 4,
        compiler_params=_params(),
        name="conv_ffn",
    )(h, h, g, wup, cw, cb, wd, gf)


def _rope_tables(S):
    half = ROT_DIM // 2
    inv_freq = 1.0 / (ROPE_THETA ** (jnp.arange(half, dtype=_F32) * 2.0 / ROT_DIM))
    ang = jnp.arange(S, dtype=_F32)[:, None] * inv_freq[None, :]
    cos, sin = jnp.cos(ang), jnp.sin(ang)
    zeros = jnp.zeros((S, HEAD_DIM - ROT_DIM), _F32)
    zh = jnp.zeros((S, half), _F32)
    ra = jnp.concatenate([cos, cos, jnp.ones_like(zeros)], axis=1)
    rb = jnp.concatenate([zh, sin, zeros], axis=1)
    rc = jnp.concatenate([-sin, zh, zeros], axis=1)
    rep = LANES // HEAD_DIM
    tile = lambda a: jnp.tile(a, (1, rep))
    return cos.T, sin.T, tile(ra), tile(rb), tile(rc)


def _layer(h, norm_mix_g, w_in, pool_w, pool_scale, w_pool_proj, w_attn_proj, w_out,
           norm_ffn_g, w_up, conv_w, conv_b, w_down, norm_out_g, tables):
    B, S, D = h.shape
    pool_width = D // 2
    attn_w = N_HEADS * HEAD_DIM
    idx_w = IDX_HEADS * IDX_DIM
    d_ff = w_down.shape[0]
    ktop = min(TOPK_MAX, S // 4)
    c0 = pool_width
    cq, ck, cv = c0, c0 + attn_w, c0 + 2 * attn_w
    ciq = c0 + 3 * attn_w
    cik = ciq + idx_w
    ciw = cik + IDX_DIM
    cg = ciw + IDX_HEADS
    cast = lambda a: a.astype(MXU_DTYPE)

    wrow = cast(jnp.concatenate(
        [w_in[:, ck:cv], w_in[:, cik:ciw], jnp.zeros((D, LANES - IDX_DIM), w_in.dtype)], axis=1))
    wt = cast(jnp.concatenate(
        [w_in[:, cq:ck], w_in[:, ciq:cik], w_in[:, cv:ciq], w_in[:, ciw:cg],
         jnp.zeros((D, 16 - IDX_HEADS), w_in.dtype)], axis=1).T)
    row1 = lambda a: a.reshape(1, -1)

    tm = TKF
    qT, iqT, vT, wT, k, ik = _projection(h, row1(norm_mix_g), wrow, wt, *tables, tm=tm)
    o = _attention(qT, iqT, vT, wT, k, ik, ktop)
    h1 = _mixer_out(h, o, row1(norm_mix_g), cast(w_in[:, :c0]), cast(w_in[:, cg:]),
                    cast(pool_w), row1(pool_scale), cast(w_pool_proj), cast(w_attn_proj),
                    cast(w_out), tm=tm)

    assert d_ff % FF_CHUNK == 0
    return _conv_ffn(h1, row1(norm_ffn_g), cast(w_up), conv_w, row1(conv_b), cast(w_down),
                     row1(norm_out_g), tm=tm)


def kernel(x, norm_mix_g, w_in, pool_w, pool_scale, w_pool_proj, w_attn_proj, w_out, norm_ffn_g,
           w_up, conv_w, conv_b, w_down, norm_final_g):
    depth = w_in.shape[0]
    assert depth == 1, "the final RMSNorm is fused into the last layer's ConvFFN kernel"
    S = x.shape[1]
    assert S % TKF == 0 and TQ == TK and TKF % TK == 0
    tables = _rope_tables(S)
    return _layer(x, norm_mix_g[0], w_in[0], pool_w[0], pool_scale[0], w_pool_proj[0],
                  w_attn_proj[0], w_out[0], norm_ffn_g[0], w_up[0], conv_w[0], conv_b[0],
                  w_down[0], norm_final_g, tables)
```

```python
import functools
import math

import jax
import jax.numpy as jnp
from jax import lax
from jax.experimental import pallas as pl
from jax.experimental.pallas import tpu as pltpu

POOL_GROUPS = 4
POOL_WINDOWS = (2, 4, 8, 16)
N_HEADS = 8
HEAD_DIM = 64
ROT_DIM = HEAD_DIM // 4
ROPE_THETA = 500000.0
IDX_HEADS = 8
IDX_DIM = 64
TOPK_MAX = 256
N_BRANCH = 2
CONV_WIDTH = 3
EPS = 1e-6

LANES = 128
SUBLANES = 8
PACK_ROWS = 16
VMEM_LIMIT_BYTES = 56 * 1024 * 1024
MXU_DTYPE = jnp.bfloat16

TQ = 256
TK = 256
TKF = 512
QK_ROWS = 256
SCORE_ROWS = 64
COUNT_ROWS = 256
LOWER_STEPS = 4
V_ROWS = HEAD_DIM + 16
LOG2E = 1.4426950408889634
POOL_HALO = 16
FFN_HALO = 16
FF_CHUNK = 256

INT_MIN = -(2 ** 31)
HALF_MIN = -(2 ** 15)
NEG_BIG = -1e30

_F32 = jnp.float32


def _rmsnorm(x, g):
    return x * lax.rsqrt(jnp.mean(x * x, axis=-1, keepdims=True) + EPS) * g


def _params(**kw):
    return pltpu.CompilerParams(vmem_limit_bytes=VMEM_LIMIT_BYTES, **kw)


def _resident(block_shape, index_map):
    return pl.BlockSpec(block_shape, index_map, pipeline_mode=pl.Buffered(1))


def _proj_kernel(x_ref, g_ref, wrow_ref, wt_ref, cos_ref, sin_ref, ra_ref, rb_ref, rc_ref,
                 qT_ref, iqT_ref, vT_ref, wT_ref, k_ref, ik_ref, *, n_vt):
    attn_w = N_HEADS * HEAD_DIM
    idx_w = IDX_HEADS * IDX_DIM
    xn = _rmsnorm(x_ref[...], g_ref[...]).astype(MXU_DTYPE)
    row = jnp.dot(xn, wrow_ref[...], preferred_element_type=_F32)
    tr = lax.dot_general(wt_ref[...], xn, (((1,), (1,)), ((), ())),
                         preferred_element_type=_F32)

    cos = cos_ref[...]
    sin = sin_ref[...]
    half = ROT_DIM // 2

    def rope_t(z, n_heads, dim):
        parts = []
        for h in range(n_heads):
            b = h * dim
            x1 = z[b:b + half]
            x2 = z[b + half:b + ROT_DIM]
            parts += [x1 * cos - x2 * sin, x2 * cos + x1 * sin, z[b + ROT_DIM:b + dim]]
        return jnp.concatenate(parts, axis=0)

    qT_ref[...] = (rope_t(tr[0:attn_w], N_HEADS, HEAD_DIM)
                   * (HEAD_DIM ** -0.5 * LOG2E)).astype(qT_ref.dtype)
    iqT_ref[...] = rope_t(tr[attn_w:attn_w + idx_w], IDX_HEADS, IDX_DIM).astype(iqT_ref.dtype)
    vt = tr[attn_w + idx_w:2 * attn_w + idx_w]
    ones_rows = (lax.broadcasted_iota(jnp.int32, (V_ROWS - HEAD_DIM, TKF), 0) == 0).astype(_F32)
    for j in range(n_vt):
        for h in range(N_HEADS):
            vT_ref[j, h, 0:HEAD_DIM, :] = vt[h * HEAD_DIM:(h + 1) * HEAD_DIM,
                                             j * TKF:(j + 1) * TKF].astype(vT_ref.dtype)
            vT_ref[j, h, HEAD_DIM:V_ROWS, :] = ones_rows.astype(vT_ref.dtype)
    wT_ref[...] = tr[2 * attn_w + idx_w:2 * attn_w + idx_w + IDX_HEADS] * (
        (IDX_HEADS ** -0.5) * (IDX_DIM ** -0.5))

    ra = ra_ref[...]
    rb = rb_ref[...]
    rc = rc_ref[...]

    def rope_rows(c):
        return (c * ra + pltpu.roll(c, half, 1) * rb + pltpu.roll(c, LANES - half, 1) * rc)

    for c in range(attn_w // LANES):
        k_ref[:, c * LANES:(c + 1) * LANES] = rope_rows(
            row[:, c * LANES:(c + 1) * LANES]).astype(k_ref.dtype)
    ik_ref[...] = rope_rows(row[:, attn_w:attn_w + LANES])[:, :IDX_DIM].astype(ik_ref.dtype)


def _projection(x, g, wrow, wt, cosT, sinT, ra, rb, rc, tm):
    B, S, D = x.shape
    attn_w = N_HEADS * HEAD_DIM
    idx_w = IDX_HEADS * IDX_DIM
    n_vt = tm // TKF
    grid = (B, S // tm)
    const = lambda b, i: (0, 0)
    out_shape = (
        jax.ShapeDtypeStruct((B, attn_w, S), MXU_DTYPE),
        jax.ShapeDtypeStruct((B, idx_w, S), MXU_DTYPE),
        jax.ShapeDtypeStruct((B, S // TKF, N_HEADS, V_ROWS, TKF), MXU_DTYPE),
        jax.ShapeDtypeStruct((B, IDX_HEADS, S), _F32),
        jax.ShapeDtypeStruct((B, S, attn_w), MXU_DTYPE),
        jax.ShapeDtypeStruct((B, S, IDX_DIM), MXU_DTYPE),
    )
    return pl.pallas_call(
        functools.partial(_proj_kernel, n_vt=n_vt),
        grid=grid,
        in_specs=[
            pl.BlockSpec((None, tm, D), lambda b, i: (b, i, 0)),
            pl.BlockSpec((1, D), const),
            pl.BlockSpec(wrow.shape, const),
            pl.BlockSpec(wt.shape, const),
            pl.BlockSpec((ROT_DIM // 2, tm), lambda b, i: (0, i)),
            pl.BlockSpec((ROT_DIM // 2, tm), lambda b, i: (0, i)),
            pl.BlockSpec((tm, LANES), lambda b, i: (i, 0)),
            pl.BlockSpec((tm, LANES), lambda b, i: (i, 0)),
            pl.BlockSpec((tm, LANES), lambda b, i: (i, 0)),
        ],
        out_specs=(
            pl.BlockSpec((None, attn_w, tm), lambda b, i: (b, 0, i)),
            pl.BlockSpec((None, idx_w, tm), lambda b, i: (b, 0, i)),
            pl.BlockSpec((None, n_vt, N_HEADS, V_ROWS, TKF), lambda b, i: (b, i, 0, 0, 0)),
            pl.BlockSpec((None, IDX_HEADS, tm), lambda b, i: (b, 0, i)),
            pl.BlockSpec((None, tm, attn_w), lambda b, i: (b, i, 0)),
            pl.BlockSpec((None, tm, IDX_DIM), lambda b, i: (b, i, 0)),
        ),
        out_shape=out_shape,
        compiler_params=_params(),
        name="proj_rope",
    )(x, g, wrow, wt, cosT, sinT, ra, rb, rc)


def _attn_kernel(iqT_ref, wT_ref, qT_ref, ik_ref, k_ref, vT_ref, o_ref,
                 sc_ref, hi_ref, gmax_ref, qpad_ref, acc_ref, m_ref, alpha_ref, bias_ref, on_ref,
                 *stage_refs, ktop, idx_bits):
    s_refs, p_refs = stage_refs[:N_HEADS], stage_refs[N_HEADS:]
    i = pl.program_id(1)
    n_rows = (i + 1) * TK

    sub_rows = lax.broadcasted_iota(jnp.int32, (SCORE_ROWS, TQ), 0)
    sub_cols = lax.broadcasted_iota(jnp.int32, (SCORE_ROWS, TQ), 1)

    def score_tile(kb, diagonal):
        k0 = pl.multiple_of(kb * TK, TK)
        for sub in range(TK // SCORE_ROWS):
            r0 = k0 + sub * SCORE_ROWS
            ikb = ik_ref[pl.ds(r0, SCORE_ROWS), :]
            score = None
            for h in range(IDX_HEADS):
                s = jnp.dot(ikb, iqT_ref[h * IDX_DIM:(h + 1) * IDX_DIM, :],
                            preferred_element_type=_F32)
                term = jnp.maximum(s, 0.0) * wT_ref[h:h + 1, :]
                score = term if score is None else score + term
            score = jnp.where(score == 0.0, 0.0, score)
            bits = pltpu.bitcast(score, jnp.int32)
            key = jnp.where(bits < 0, bits ^ jnp.int32(0x7FFFFFFF), bits)
            if diagonal:
                causal = sub_rows + sub * SCORE_ROWS <= sub_cols
                key = jnp.where(causal, key, jnp.int32(INT_MIN))
            sc_ref[pl.ds(r0, SCORE_ROWS), :] = key
            upper = jnp.right_shift(key, 16).astype(hi_ref.dtype)
            hi_ref[pl.ds(r0, SCORE_ROWS), :] = upper
            rows = slice(sub * SCORE_ROWS, (sub + 1) * SCORE_ROWS)
            seen = gmax_ref[rows, :]
            gmax_ref[rows, :] = jnp.where(upper > seen, upper, seen)

    def score_body(kb, carry):
        score_tile(kb, False)
        return carry

    gmax_ref[...] = jnp.full(gmax_ref.shape, HALF_MIN, gmax_ref.dtype)
    lax.fori_loop(0, i, score_body, 0)
    score_tile(i, True)

    n_pairs = (i + TKF // TK) // (TKF // TK)

    @pl.when((i + 1) % (TKF // TK) != 0)
    def _pad_keys():
        sc_ref[pl.ds(pl.multiple_of(n_rows, TK), TK), :] = jnp.full((TK, TQ), INT_MIN, jnp.int32)

    row8_iota = lax.broadcasted_iota(jnp.int32, (SUBLANES, TQ), 0)

    def count(pred):
        def body(r, cnt):
            r0 = pl.multiple_of(r * COUNT_ROWS, COUNT_ROWS)
            blk = sc_ref[pl.ds(r0, COUNT_ROWS), :]
            parts = []
            for j in range(COUNT_ROWS // SUBLANES):
                parts.append(pred(blk[j * SUBLANES:(j + 1) * SUBLANES],
                                  r0 + j * SUBLANES + row8_iota).astype(jnp.int32))
            while len(parts) > 1:
                parts = [parts[a] + parts[a + 1] for a in range(0, len(parts), 2)]
            return cnt + parts[0]
        cnt = lax.fori_loop(0, n_rows // COUNT_ROWS, body, jnp.zeros((SUBLANES, TQ), jnp.int32))
        return jnp.sum(cnt.astype(_F32), axis=0, keepdims=True)

    def step(tb, cge, crej, cand_b, cnt):
        ok = cnt >= ktop
        return (jnp.where(ok, cand_b, tb), jnp.where(ok, cnt, cge), jnp.where(ok, crej, cnt))

    def count_upper(cand_hi):
        cand16 = jnp.broadcast_to(cand_hi, (PACK_ROWS, TQ)).astype(hi_ref.dtype)
        one16 = jnp.ones((PACK_ROWS, TQ), hi_ref.dtype)
        zero16 = jnp.zeros((PACK_ROWS, TQ), hi_ref.dtype)

        def body(r, cnt):
            r0 = pl.multiple_of(r * COUNT_ROWS, COUNT_ROWS)
            blk = hi_ref[pl.ds(r0, COUNT_ROWS), :]
            parts = [jnp.where(blk[j * PACK_ROWS:(j + 1) * PACK_ROWS] >= cand16, one16, zero16)
                     for j in range(COUNT_ROWS // PACK_ROWS)]
            while len(parts) > 1:
                parts = [parts[a] + parts[a + 1] for a in range(0, len(parts), 2)]
            return cnt + parts[0].astype(jnp.int32)
        cnt = lax.fori_loop(0, n_rows // COUNT_ROWS, body, jnp.zeros((PACK_ROWS, TQ), jnp.int32))
        return jnp.sum(cnt.astype(_F32), axis=0, keepdims=True)

    gmax = gmax_ref[...]
    parts_lo = [gmax[j * PACK_ROWS:(j + 1) * PACK_ROWS] for j in range(TK // PACK_ROWS)]
    parts_hi = list(parts_lo)
    while len(parts_lo) > 1:
        parts_lo = [jnp.where(parts_lo[a] < parts_lo[a + 1], parts_lo[a], parts_lo[a + 1])
                    for a in range(0, len(parts_lo), 2)]
        parts_hi = [jnp.where(parts_hi[a] > parts_hi[a + 1], parts_hi[a], parts_hi[a + 1])
                    for a in range(0, len(parts_hi), 2)]
    lane = lax.broadcasted_iota(jnp.int32, (1, TQ), 1)
    enough_keys = i * TQ + lane + 1 >= max(ktop, TK)
    lo0 = jnp.min(parts_lo[0].astype(_F32), axis=0, keepdims=True).astype(jnp.int32)
    hi0 = jnp.max(parts_hi[0].astype(_F32), axis=0, keepdims=True).astype(jnp.int32) + 1
    lo0 = jnp.where(enough_keys, lo0, HALF_MIN + 1)
    hi0 = jnp.where(enough_keys, hi0, -HALF_MIN)

    width = jnp.max((hi0 - lo0).astype(_F32)).astype(jnp.int32)
    n_upper = jnp.int32(0)
    for b in range(17):
        n_upper = n_upper + (width > (1 << b)).astype(jnp.int32)

    def upper_body(it, carry):
        lo, hi, cge, crej = carry
        mid = lo + jnp.right_shift(hi - lo, 1)
        cnt = count_upper(mid)
        ok = cnt >= ktop
        return (jnp.where(ok, mid, lo), jnp.where(ok, hi, mid),
                jnp.where(ok, cnt, cge), jnp.where(ok, crej, cnt))

    zeros_q = jnp.zeros((1, TQ), jnp.int32)
    zeros_f = jnp.zeros((1, TQ), _F32)
    thr_hi, _, cge, crej = lax.fori_loop(0, n_upper, upper_body,
                                         (lo0, hi0, count_upper(lo0), zeros_f))
    has_upper = cge >= ktop
    tb = jnp.where(has_upper, jnp.left_shift(thr_hi, 16) ^ jnp.int32(INT_MIN), 0)
    cge = jnp.where(has_upper, cge, 0.0)

    above = crej
    half_lo = HALF_MIN

    def lower_plane(r, carry):
        r0 = pl.multiple_of(r * COUNT_ROWS, COUNT_ROWS)
        key = sc_ref[pl.ds(r0, COUNT_ROWS), :]
        low = jnp.where(jnp.right_shift(key, 16) == thr_hi, (key & 0xFFFF) + half_lo, half_lo)
        hi_ref[pl.ds(r0, COUNT_ROWS), :] = low.astype(hi_ref.dtype)
        return carry

    lax.fori_loop(0, n_rows // COUNT_ROWS, lower_plane, 0)

    lane = lax.broadcasted_iota(jnp.int32, (1, TQ), 1)
    enough_keys = i * TQ + lane + 1 >= ktop

    def unsettled(cge):
        return jnp.max(jnp.where(enough_keys & (cge != ktop), 1.0, 0.0))

    def lower_cond(carry):
        return (carry[0] < 32) & (carry[4] > 0.0)

    def lower_body(carry):
        it, tb, cge, crej, _ = carry
        for _ in range(LOWER_STEPS):
            cand_b = tb | jnp.left_shift(jnp.int32(1), 31 - it)
            cnt = above + count_upper((cand_b & 0xFFFF) + half_lo)
            tb, cge, crej = step(tb, cge, crej, cand_b, cnt)
            it = it + 1
        return it, tb, cge, crej, unsettled(cge)

    _, tb, cge, crej, _ = lax.while_loop(lower_cond, lower_body,
                                         (jnp.int32(16), tb, cge, crej, unsettled(cge)))
    thr = tb ^ jnp.int32(INT_MIN)
    has_thr = tb != 0

    need = ktop - crej
    any_tie = jnp.max(jnp.where(has_thr & (cge > ktop), 1.0, 0.0)) > 0.0

    @pl.when(any_tie)
    def _demote_surplus_ties():
        def body(it, p):
            cand = p | jnp.left_shift(jnp.int32(1), idx_bits - 1 - it)
            cnt = count(lambda blk, idx: (blk == thr) & (idx < cand))
            return jnp.where(cnt < need, cand, p)
        tie_last = lax.fori_loop(0, idx_bits, body, zeros_q)
        blk_iota = lax.broadcasted_iota(jnp.int32, (COUNT_ROWS, TQ), 0)

        def demote(r, carry):
            r0 = pl.multiple_of(r * COUNT_ROWS, COUNT_ROWS)
            blk = sc_ref[pl.ds(r0, COUNT_ROWS), :]
            surplus = has_thr & (blk == thr) & ((r0 + blk_iota) > tie_last)
            sc_ref[pl.ds(r0, COUNT_ROWS), :] = jnp.where(surplus, thr - 1, blk)
            return carry
        lax.fori_loop(0, n_rows // COUNT_ROWS, demote, 0)

    thr_ge = jnp.where(has_thr, thr, jnp.int32(INT_MIN + 1))

    pair_rows = lax.broadcasted_iota(jnp.int32, (2 * HEAD_DIM, TQ), 0)
    for h in range(N_HEADS):
        pair = qT_ref[(h // 2) * 2 * HEAD_DIM:(h // 2 + 1) * 2 * HEAD_DIM, :].astype(_F32)
        mine = (pair_rows >= HEAD_DIM) if h % 2 else (pair_rows < HEAD_DIM)
        qpad_ref[h] = jnp.where(mine, pair, 0.0).astype(qpad_ref.dtype)
    m_ref[...] = jnp.full(m_ref.shape, NEG_BIG, _F32)
    acc_ref[...] = jnp.zeros(acc_ref.shape, _F32)
    alpha_ref[...] = jnp.ones(alpha_ref.shape, _F32)
    for h in range(N_HEADS):
        p_refs[h][...] = jnp.zeros(p_refs[h].shape, p_refs[h].dtype)

    def apply_pv(kb):
        for h in range(N_HEADS):
            pv = jnp.dot(vT_ref[kb, h], p_refs[h][0], preferred_element_type=_F32)
            acc_ref[h] = alpha_ref[h:h + 1, :] * acc_ref[h] + pv

    def flash_body(kb, carry):
        apply_pv(jnp.maximum(kb - 1, 0))
        k0 = pl.multiple_of(kb * TKF, TKF)
        bias_ref[...] = jnp.where(sc_ref[pl.ds(k0, TKF), :] >= thr_ge, 0.0, NEG_BIG)
        for h in range(N_HEADS):
            for r in range(TKF // QK_ROWS):
                kpair = k_ref[pl.ds(k0 + r * QK_ROWS, QK_ROWS),
                              (h // 2) * 2 * HEAD_DIM:(h // 2 + 1) * 2 * HEAD_DIM]
                s_refs[h][0, r * QK_ROWS:(r + 1) * QK_ROWS, :] = jnp.dot(
                    kpair, qpad_ref[h], preferred_element_type=_F32)
        for h in range(N_HEADS):
            col_max = [None] * 4
            for j in range(TKF // SUBLANES):
                rows = slice(j * SUBLANES, (j + 1) * SUBLANES)
                slab = s_refs[h][0, rows, :] + bias_ref[rows, :]
                c = j % len(col_max)
                col_max[c] = slab if col_max[c] is None else jnp.maximum(col_max[c], slab)
            tile_max = jnp.maximum(jnp.maximum(col_max[0], col_max[1]),
                                   jnp.maximum(col_max[2], col_max[3]))
            m_old = m_ref[h:h + 1, :]
            m_new = jnp.maximum(m_old, jnp.max(tile_max, axis=0, keepdims=True))
            alpha_ref[h:h + 1, :] = jnp.exp2(m_old - m_new)
            m_ref[h:h + 1, :] = m_new
        for h in range(N_HEADS):
            p_refs[h][0] = jnp.exp2(s_refs[h][0] + bias_ref[...]
                                    - m_ref[h:h + 1, :]).astype(p_refs[h].dtype)
        return carry

    lax.fori_loop(0, n_pairs, flash_body, 0)
    apply_pv(n_pairs - 1)

    for h in range(N_HEADS):
        on_ref[h * HEAD_DIM:(h + 1) * HEAD_DIM, :] = (
            acc_ref[h, 0:HEAD_DIM, :] / acc_ref[h, HEAD_DIM:HEAD_DIM + 1, :])
    o_ref[...] = on_ref[...].T.astype(o_ref.dtype)


def _attention(qT, iqT, vT, wT, k, ik, ktop):
    B, attn_w, S = qT.shape
    idx_w = iqT.shape[1]
    idx_bits = max(1, math.ceil(math.log2(S)))
    grid = (B, S // TQ)
    return pl.pallas_call(
        functools.partial(_attn_kernel, ktop=ktop, idx_bits=idx_bits),
        grid=grid,
        in_specs=[
            pl.BlockSpec((None, idx_w, TQ), lambda b, i: (b, 0, i)),
            pl.BlockSpec((None, IDX_HEADS, TQ), lambda b, i: (b, 0, i)),
            pl.BlockSpec((None, attn_w, TQ), lambda b, i: (b, 0, i)),
            _resident((None, S, IDX_DIM), lambda b, i: (b, 0, 0)),
            _resident((None, S, attn_w), lambda b, i: (b, 0, 0)),
            _resident((None, S // TKF, N_HEADS, V_ROWS, TKF), lambda b, i: (b, 0, 0, 0, 0)),
        ],
        out_specs=pl.BlockSpec((None, TQ, attn_w), lambda b, i: (b, i, 0)),
        out_shape=jax.ShapeDtypeStruct((B, S, attn_w), MXU_DTYPE),
        scratch_shapes=[
            pltpu.VMEM((S, TQ), jnp.int32),
            pltpu.VMEM((S, TQ), jnp.int16),
            pltpu.VMEM((TK, TQ), jnp.int16),
            pltpu.VMEM((N_HEADS, 2 * HEAD_DIM, TQ), MXU_DTYPE),
            pltpu.VMEM((N_HEADS, V_ROWS, TQ), _F32),
            pltpu.VMEM((N_HEADS, TQ), _F32),
            pltpu.VMEM((N_HEADS, TQ), _F32),
            pltpu.VMEM((TKF, TQ), _F32),
            pltpu.VMEM((attn_w, TQ), _F32),
        ] + [pltpu.VMEM((1, TKF, TQ), _F32)] * N_HEADS
          + [pltpu.VMEM((1, TKF, TQ), MXU_DTYPE)] * N_HEADS,
        compiler_params=_params(),
        name="topk_attention",
    )(iqT, wT, qT, ik, k, vT)


def _mixer_kernel(x_ref, halo_ref, o_ref, g_ref, wpool_ref, wgate_ref, poolw_ref, pscale_ref,
                  wpp_ref, wap_ref, wout_ref, h_ref, u_ref, *, tm):
    i = pl.program_id(1)
    D = x_ref.shape[-1]
    group_dim = wpool_ref.shape[1] // POOL_GROUPS
    g = g_ref[...]
    x = x_ref[...]
    xn = _rmsnorm(x, g).astype(MXU_DTYPE)
    hn = _rmsnorm(halo_ref[...], g).astype(MXU_DTYPE)
    u_halo = jnp.dot(hn, wpool_ref[...], preferred_element_type=_F32)
    u_ref[0:POOL_HALO, :] = jnp.where(i > 0, u_halo, 0.0)
    u_ref[POOL_HALO:POOL_HALO + tm, :] = jnp.dot(xn, wpool_ref[...], preferred_element_type=_F32)

    t = i * tm + lax.broadcasted_iota(jnp.int32, (tm, group_dim), 0)
    mixed = []
    for gi, w in enumerate(POOL_WINDOWS):
        lanes = slice(gi * group_dim, (gi + 1) * group_dim)
        cur = u_ref[POOL_HALO:POOL_HALO + tm, lanes]
        sums = cur
        for j in range(1, w):
            sums = sums + u_ref[POOL_HALO - j:POOL_HALO - j + tm, lanes]
        cnt = jnp.minimum(t + 1, w).astype(_F32)
        pooled = sums / cnt - cur
        mixed.append(jnp.dot(pooled.astype(MXU_DTYPE), poolw_ref[gi], preferred_element_type=_F32))
    mixed = jnp.concatenate(mixed, axis=-1) * pscale_ref[...]
    y_pool = jnp.dot(mixed.astype(MXU_DTYPE), wpp_ref[...], preferred_element_type=_F32)
    y_attn = jnp.dot(o_ref[...], wap_ref[...], preferred_element_type=_F32)
    gates = jax.nn.sigmoid(jnp.dot(xn, wgate_ref[...], preferred_element_type=_F32))
    merged = gates[:, :D] * y_pool + gates[:, D:] * y_attn
    h_ref[...] = x + jnp.dot(merged.astype(MXU_DTYPE), wout_ref[...], preferred_element_type=_F32)


def _mixer_out(x, o, g, wpool, wgate, poolw, pscale, wpp, wap, wout, tm):
    B, S, D = x.shape
    pool_w = wpool.shape[1]
    attn_w = o.shape[-1]
    grid = (B, S // tm)
    const2 = lambda b, i: (0, 0)
    halo_blocks = tm // POOL_HALO
    return pl.pallas_call(
        functools.partial(_mixer_kernel, tm=tm),
        grid=grid,
        in_specs=[
            pl.BlockSpec((None, tm, D), lambda b, i: (b, i, 0)),
            pl.BlockSpec((None, POOL_HALO, D),
                         lambda b, i: (b, jnp.maximum(i * halo_blocks - 1, 0), 0)),
            pl.BlockSpec((None, tm, attn_w), lambda b, i: (b, i, 0)),
            pl.BlockSpec((1, D), const2),
            pl.BlockSpec(wpool.shape, const2),
            pl.BlockSpec(wgate.shape, const2),
            pl.BlockSpec(poolw.shape, lambda b, i: (0, 0, 0)),
            pl.BlockSpec((1, pool_w), const2),
            pl.BlockSpec(wpp.shape, const2),
            pl.BlockSpec(wap.shape, const2),
            pl.BlockSpec(wout.shape, const2),
        ],
        out_specs=pl.BlockSpec((None, tm, D), lambda b, i: (b, i, 0)),
        out_shape=jax.ShapeDtypeStruct((B, S, D), _F32),
        scratch_shapes=[pltpu.VMEM((POOL_HALO + tm, pool_w), _F32)],
        compiler_params=_params(),
        name="mixer_out",
    )(x, x, o, g, wpool, wgate, poolw, pscale, wpp, wap, wout)


def _ffn_kernel(h_ref, halo_ref, g_ref, wup_ref, cw_ref, cb_ref, wd_ref, gf_ref, out_ref,
                hn_ref, act_ref, ua0_ref, ua1_ref, ub0_ref, ub1_ref, *, tm, d_ff):
    ua_refs, ub_refs = (ua0_ref, ua1_ref), (ub0_ref, ub1_ref)
    i = pl.program_id(1)
    g = g_ref[...]
    h = h_ref[...]
    hn_ref[0:FFN_HALO, :] = _rmsnorm(halo_ref[...], g).astype(MXU_DTYPE)
    hn_ref[FFN_HALO:FFN_HALO + tm, :] = _rmsnorm(h, g).astype(MXU_DTYPE)
    first = i == 0

    def conv(u_ref, up, cw, cb):
        u_ref[FFN_HALO:FFN_HALO + tm, :] = up[FFN_HALO:, :]
        u_ref[0:FFN_HALO, :] = jnp.where(first, 0.0, up[0:FFN_HALO, :])
        out = cb
        for j in range(CONV_WIDTH):
            off = FFN_HALO - (CONV_WIDTH - 1) + j
            out = out + u_ref[off:off + tm, :] * cw[j:j + 1, :]
        return out

    for c in range(d_ff // FF_CHUNK):
        hn = hn_ref[...]
        cols_a = slice(c * FF_CHUNK, (c + 1) * FF_CHUNK)
        cols_b = slice(d_ff + c * FF_CHUNK, d_ff + (c + 1) * FF_CHUNK)
        a = conv(ua_refs[c % 2], jnp.dot(hn, wup_ref[:, cols_a], preferred_element_type=_F32),
                 cw_ref[:, cols_a], cb_ref[:, cols_a])
        b = conv(ub_refs[c % 2], jnp.dot(hn, wup_ref[:, cols_b], preferred_element_type=_F32),
                 cw_ref[:, cols_b], cb_ref[:, cols_b])
        act_ref[:, cols_a] = (a * jax.nn.sigmoid(a) * b).astype(MXU_DTYPE)
    down = jnp.dot(act_ref[...], wd_ref[...], preferred_element_type=_F32)
    out_ref[...] = _rmsnorm(h + down, gf_ref[...])


def _conv_ffn(h, g, wup, cw, cb, wd, gf, tm):
    B, S, D = h.shape
    d_ff = wd.shape[0]
    grid = (B, S // tm)
    const2 = lambda b, i: (0, 0)
    halo_blocks = tm // FFN_HALO
    return pl.pallas_call(
        functools.partial(_ffn_kernel, tm=tm, d_ff=d_ff),
        grid=grid,
        in_specs=[
            pl.BlockSpec((None, tm, D), lambda b, i: (b, i, 0)),
            pl.BlockSpec((None, FFN_HALO, D),
                         lambda b, i: (b, jnp.maximum(i * halo_blocks - 1, 0), 0)),
            pl.BlockSpec((1, D), const2),
            _resident(wup.shape, const2),
            pl.BlockSpec(cw.shape, const2),
            pl.BlockSpec(cb.shape, const2),
            _resident(wd.shape, const2),
            pl.BlockSpec((1, D), const2),
        ],
        out_specs=pl.BlockSpec((None, tm, D), lambda b, i: (b, i, 0)),
        out_shape=jax.ShapeDtypeStruct((B, S, D), _F32),
        scratch_shapes=[
            pltpu.VMEM((FFN_HALO + tm, D), MXU_DTYPE),
            pltpu.VMEM((tm, d_ff), MXU_DTYPE),
        ] + [pltpu.VMEM((FFN_HALO + tm, FF_CHUNK), _F32)] * 4,
        compiler_params=_params(),
        name="conv_ffn",
    )(h, h, g, wup, cw, cb, wd, gf)


def _rope_tables(S):
    half = ROT_DIM // 2
    inv_freq = 1.0 / (ROPE_THETA ** (jnp.arange(half, dtype=_F32) * 2.0 / ROT_DIM))
    ang = jnp.arange(S, dtype=_F32)[:, None] * inv_freq[None, :]
    cos, sin = jnp.cos(ang), jnp.sin(ang)
    zeros = jnp.zeros((S, HEAD_DIM - ROT_DIM), _F32)
    zh = jnp.zeros((S, half), _F32)
    ra = jnp.concatenate([cos, cos, jnp.ones_like(zeros)], axis=1)
    rb = jnp.concatenate([zh, sin, zeros], axis=1)
    rc = jnp.concatenate([-sin, zh, zeros], axis=1)
    rep = LANES // HEAD_DIM
    tile = lambda a: jnp.tile(a, (1, rep))
    return cos.T, sin.T, tile(ra), tile(rb), tile(rc)


def _layer(h, norm_mix_g, w_in, pool_w, pool_scale, w_pool_proj, w_attn_proj, w_out,
           norm_ffn_g, w_up, conv_w, conv_b, w_down, norm_out_g, tables):
    B, S, D = h.shape
    pool_width = D // 2
    attn_w = N_HEADS * HEAD_DIM
    idx_w = IDX_HEADS * IDX_DIM
    d_ff = w_down.shape[0]
    ktop = min(TOPK_MAX, S // 4)
    c0 = pool_width
    cq, ck, cv = c0, c0 + attn_w, c0 + 2 * attn_w
    ciq = c0 + 3 * attn_w
    cik = ciq + idx_w
    ciw = cik + IDX_DIM
    cg = ciw + IDX_HEADS
    cast = lambda a: a.astype(MXU_DTYPE)

    wrow = cast(jnp.concatenate(
        [w_in[:, ck:cv], w_in[:, cik:ciw], jnp.zeros((D, LANES - IDX_DIM), w_in.dtype)], axis=1))
    wt = cast(jnp.concatenate(
        [w_in[:, cq:ck], w_in[:, ciq:cik], w_in[:, cv:ciq], w_in[:, ciw:cg],
         jnp.zeros((D, 16 - IDX_HEADS), w_in.dtype)], axis=1).T)
    row1 = lambda a: a.reshape(1, -1)

    tm = TKF
    qT, iqT, vT, wT, k, ik = _projection(h, row1(norm_mix_g), wrow, wt, *tables, tm=tm)
    o = _attention(qT, iqT, vT, wT, k, ik, ktop)
    h1 = _mixer_out(h, o, row1(norm_mix_g), cast(w_in[:, :c0]), cast(w_in[:, cg:]),
                    cast(pool_w), row1(pool_scale), cast(w_pool_proj), cast(w_attn_proj),
                    cast(w_out), tm=tm)

    assert d_ff % FF_CHUNK == 0
    return _conv_ffn(h1, row1(norm_ffn_g), cast(w_up), conv_w, row1(conv_b), cast(w_down),
                     row1(norm_out_g), tm=tm)


def kernel(x, norm_mix_g, w_in, pool_w, pool_scale, w_pool_proj, w_attn_proj, w_out, norm_ffn_g,
           w_up, conv_w, conv_b, w_down, norm_final_g):
    depth = w_in.shape[0]
    assert depth == 1, "the final RMSNorm is fused into the last layer's ConvFFN kernel"
    S = x.shape[1]
    assert S % TKF == 0 and TQ == TK and TKF % TK == 0
    tables = _rope_tables(S)
    return _layer(x, norm_mix_g[0], w_in[0], pool_w[0], pool_scale[0], w_pool_proj[0],
                  w_attn_proj[0], w_out[0], norm_ffn_g[0], w_up[0], conv_w[0], conv_b[0],
                  w_down[0], norm_final_g, tables)
```

```python
import functools
import math

import jax
import jax.numpy as jnp
from jax import lax
from jax.experimental import pallas as pl
from jax.experimental.pallas import tpu as pltpu

POOL_GROUPS = 4
POOL_WINDOWS = (2, 4, 8, 16)
N_HEADS = 8
HEAD_DIM = 64
ROT_DIM = HEAD_DIM // 4
ROPE_THETA = 500000.0
IDX_HEADS = 8
IDX_DIM = 64
TOPK_MAX = 256
N_BRANCH = 2
CONV_WIDTH = 3
EPS = 1e-6

LANES = 128
SUBLANES = 8
PACK_ROWS = 16
VMEM_LIMIT_BYTES = 56 * 1024 * 1024
MXU_DTYPE = jnp.bfloat16

TQ = 256
TK = 256
TKF = 512
QK_ROWS = 256
SCORE_ROWS = 64
COUNT_ROWS = 256
LOWER_STEPS = 4
V_ROWS = HEAD_DIM + 16
LOG2E = 1.4426950408889634
POOL_HALO = 16
FFN_HALO = 16
FF_CHUNK = 2816

INT_MIN = -(2 ** 31)
NEG_BIG = -1e30

_F32 = jnp.float32


def _rmsnorm(x, g):
    return x * lax.rsqrt(jnp.mean(x * x, axis=-1, keepdims=True) + EPS) * g


def _params(**kw):
    return pltpu.CompilerParams(vmem_limit_bytes=VMEM_LIMIT_BYTES, **kw)


def _resident(block_shape, index_map):
    return pl.BlockSpec(block_shape, index_map, pipeline_mode=pl.Buffered(1))


def _proj_kernel(x_ref, g_ref, wrow_ref, wt_ref, cos_ref, sin_ref, ra_ref, rb_ref, rc_ref,
                 qT_ref, iqT_ref, vT_ref, wT_ref, k_ref, ik_ref, *, n_vt):
    attn_w = N_HEADS * HEAD_DIM
    idx_w = IDX_HEADS * IDX_DIM
    xn = _rmsnorm(x_ref[...], g_ref[...]).astype(MXU_DTYPE)
    row = jnp.dot(xn, wrow_ref[...], preferred_element_type=_F32)
    tr = lax.dot_general(wt_ref[...], xn, (((1,), (1,)), ((), ())),
                         preferred_element_type=_F32)

    cos = cos_ref[...]
    sin = sin_ref[...]
    half = ROT_DIM // 2

    def rope_t(z, n_heads, dim):
        parts = []
        for h in range(n_heads):
            b = h * dim
            x1 = z[b:b + half]
            x2 = z[b + half:b + ROT_DIM]
            parts += [x1 * cos - x2 * sin, x2 * cos + x1 * sin, z[b + ROT_DIM:b + dim]]
        return jnp.concatenate(parts, axis=0)

    qT_ref[...] = (rope_t(tr[0:attn_w], N_HEADS, HEAD_DIM)
                   * (HEAD_DIM ** -0.5 * LOG2E)).astype(qT_ref.dtype)
    iqT_ref[...] = rope_t(tr[attn_w:attn_w + idx_w], IDX_HEADS, IDX_DIM).astype(iqT_ref.dtype)
    vt = tr[attn_w + idx_w:2 * attn_w + idx_w]
    ones_rows = (lax.broadcasted_iota(jnp.int32, (V_ROWS - HEAD_DIM, TKF), 0) == 0).astype(_F32)
    for j in range(n_vt):
        for h in range(N_HEADS):
            vT_ref[j, h, 0:HEAD_DIM, :] = vt[h * HEAD_DIM:(h + 1) * HEAD_DIM,
                                             j * TKF:(j + 1) * TKF].astype(vT_ref.dtype)
            vT_ref[j, h, HEAD_DIM:V_ROWS, :] = ones_rows.astype(vT_ref.dtype)
    wT_ref[...] = tr[2 * attn_w + idx_w:2 * attn_w + idx_w + IDX_HEADS] * (
        (IDX_HEADS ** -0.5) * (IDX_DIM ** -0.5))

    ra = ra_ref[...]
    rb = rb_ref[...]
    rc = rc_ref[...]

    def rope_rows(c):
        return (c * ra + pltpu.roll(c, half, 1) * rb + pltpu.roll(c, LANES - half, 1) * rc)

    for c in range(attn_w // LANES):
        k_ref[:, c * LANES:(c + 1) * LANES] = rope_rows(
            row[:, c * LANES:(c + 1) * LANES]).astype(k_ref.dtype)
    ik_ref[...] = rope_rows(row[:, attn_w:attn_w + LANES])[:, :IDX_DIM].astype(ik_ref.dtype)


def _projection(x, g, wrow, wt, cosT, sinT, ra, rb, rc, tm):
    B, S, D = x.shape
    attn_w = N_HEADS * HEAD_DIM
    idx_w = IDX_HEADS * IDX_DIM
    n_vt = tm // TKF
    grid = (B, S // tm)
    const = lambda b, i: (0, 0)
    out_shape = (
        jax.ShapeDtypeStruct((B, attn_w, S), MXU_DTYPE),
        jax.ShapeDtypeStruct((B, idx_w, S), MXU_DTYPE),
        jax.ShapeDtypeStruct((B, S // TKF, N_HEADS, V_ROWS, TKF), MXU_DTYPE),
        jax.ShapeDtypeStruct((B, IDX_HEADS, S), _F32),
        jax.ShapeDtypeStruct((B, S, attn_w), MXU_DTYPE),
        jax.ShapeDtypeStruct((B, S, IDX_DIM), MXU_DTYPE),
    )
    return pl.pallas_call(
        functools.partial(_proj_kernel, n_vt=n_vt),
        grid=grid,
        in_specs=[
            pl.BlockSpec((None, tm, D), lambda b, i: (b, i, 0)),
            pl.BlockSpec((1, D), const),
            pl.BlockSpec(wrow.shape, const),
            pl.BlockSpec(wt.shape, const),
            pl.BlockSpec((ROT_DIM // 2, tm), lambda b, i: (0, i)),
            pl.BlockSpec((ROT_DIM // 2, tm), lambda b, i: (0, i)),
            pl.BlockSpec((tm, LANES), lambda b, i: (i, 0)),
            pl.BlockSpec((tm, LANES), lambda b, i: (i, 0)),
            pl.BlockSpec((tm, LANES), lambda b, i: (i, 0)),
        ],
        out_specs=(
            pl.BlockSpec((None, attn_w, tm), lambda b, i: (b, 0, i)),
            pl.BlockSpec((None, idx_w, tm), lambda b, i: (b, 0, i)),
            pl.BlockSpec((None, n_vt, N_HEADS, V_ROWS, TKF), lambda b, i: (b, i, 0, 0, 0)),
            pl.BlockSpec((None, IDX_HEADS, tm), lambda b, i: (b, 0, i)),
            pl.BlockSpec((None, tm, attn_w), lambda b, i: (b, i, 0)),
            pl.BlockSpec((None, tm, IDX_DIM), lambda b, i: (b, i, 0)),
        ),
        out_shape=out_shape,
        compiler_params=_params(),
        name="proj_rope",
    )(x, g, wrow, wt, cosT, sinT, ra, rb, rc)


def _attn_kernel(iqT_ref, wT_ref, qT_ref, ik_ref, k_ref, vT_ref, o_ref,
                 sc_ref, hi_ref, qpad_ref, acc_ref, m_ref, alpha_ref, bias_ref, on_ref, *stage_refs,
                 ktop, idx_bits):
    s_refs, p_refs = stage_refs[:N_HEADS], stage_refs[N_HEADS:]
    i = pl.program_id(1)
    n_rows = (i + 1) * TK

    sub_rows = lax.broadcasted_iota(jnp.int32, (SCORE_ROWS, TQ), 0)
    sub_cols = lax.broadcasted_iota(jnp.int32, (SCORE_ROWS, TQ), 1)

    def score_tile(kb, diagonal):
        k0 = pl.multiple_of(kb * TK, TK)
        for sub in range(TK // SCORE_ROWS):
            r0 = k0 + sub * SCORE_ROWS
            ikb = ik_ref[pl.ds(r0, SCORE_ROWS), :]
            score = None
            for h in range(IDX_HEADS):
                s = jnp.dot(ikb, iqT_ref[h * IDX_DIM:(h + 1) * IDX_DIM, :],
                            preferred_element_type=_F32)
                term = jnp.maximum(s, 0.0) * wT_ref[h:h + 1, :]
                score = term if score is None else score + term
            score = jnp.where(score == 0.0, 0.0, score)
            bits = pltpu.bitcast(score, jnp.int32)
            key = jnp.where(bits < 0, bits ^ jnp.int32(0x7FFFFFFF), bits)
            if diagonal:
                causal = sub_rows + sub * SCORE_ROWS <= sub_cols
                key = jnp.where(causal, key, jnp.int32(INT_MIN))
            sc_ref[pl.ds(r0, SCORE_ROWS), :] = key
            hi_ref[pl.ds(r0, SCORE_ROWS), :] = jnp.right_shift(key, 16).astype(hi_ref.dtype)

    def score_body(kb, carry):
        score_tile(kb, False)
        return carry

    lax.fori_loop(0, i, score_body, 0)
    score_tile(i, True)

    n_pairs = (i + TKF // TK) // (TKF // TK)

    @pl.when((i + 1) % (TKF // TK) != 0)
    def _pad_keys():
        sc_ref[pl.ds(pl.multiple_of(n_rows, TK), TK), :] = jnp.full((TK, TQ), INT_MIN, jnp.int32)

    row8_iota = lax.broadcasted_iota(jnp.int32, (SUBLANES, TQ), 0)

    def count(pred):
        def body(r, cnt):
            r0 = pl.multiple_of(r * COUNT_ROWS, COUNT_ROWS)
            blk = sc_ref[pl.ds(r0, COUNT_ROWS), :]
            parts = []
            for j in range(COUNT_ROWS // SUBLANES):
                parts.append(pred(blk[j * SUBLANES:(j + 1) * SUBLANES],
                                  r0 + j * SUBLANES + row8_iota).astype(jnp.int32))
            while len(parts) > 1:
                parts = [parts[a] + parts[a + 1] for a in range(0, len(parts), 2)]
            return cnt + parts[0]
        cnt = lax.fori_loop(0, n_rows // COUNT_ROWS, body, jnp.zeros((SUBLANES, TQ), jnp.int32))
        return jnp.sum(cnt.astype(_F32), axis=0, keepdims=True)

    def step(tb, cge, crej, cand_b, cnt):
        ok = cnt >= ktop
        return (jnp.where(ok, cand_b, tb), jnp.where(ok, cnt, cge), jnp.where(ok, crej, cnt))

    def count_upper(cand_hi):
        cand16 = jnp.broadcast_to(cand_hi, (PACK_ROWS, TQ)).astype(hi_ref.dtype)
        one16 = jnp.ones((PACK_ROWS, TQ), hi_ref.dtype)
        zero16 = jnp.zeros((PACK_ROWS, TQ), hi_ref.dtype)

        def body(r, cnt):
            r0 = pl.multiple_of(r * COUNT_ROWS, COUNT_ROWS)
            blk = hi_ref[pl.ds(r0, COUNT_ROWS), :]
            parts = [jnp.where(blk[j * PACK_ROWS:(j + 1) * PACK_ROWS] >= cand16, one16, zero16)
                     for j in range(COUNT_ROWS // PACK_ROWS)]
            while len(parts) > 1:
                parts = [parts[a] + parts[a + 1] for a in range(0, len(parts), 2)]
            return cnt + parts[0].astype(jnp.int32)
        cnt = lax.fori_loop(0, n_rows // COUNT_ROWS, body, jnp.zeros((PACK_ROWS, TQ), jnp.int32))
        return jnp.sum(cnt.astype(_F32), axis=0, keepdims=True)

    def upper_body(it, carry):
        tb = carry[0]
        cand_b = tb | jnp.left_shift(jnp.int32(1), 31 - it)
        cand = cand_b ^ jnp.int32(INT_MIN)
        return step(*carry, cand_b, count_upper(jnp.right_shift(cand, 16)))

    zeros_q = jnp.zeros((1, TQ), jnp.int32)
    zeros_f = jnp.zeros((1, TQ), _F32)
    tb, cge, crej = lax.fori_loop(0, 16, upper_body, (zeros_q, zeros_f, zeros_f))

    above = crej
    thr_hi = jnp.right_shift(tb ^ jnp.int32(INT_MIN), 16)
    half_lo = -(1 << 15)

    def lower_plane(r, carry):
        r0 = pl.multiple_of(r * COUNT_ROWS, COUNT_ROWS)
        key = sc_ref[pl.ds(r0, COUNT_ROWS), :]
        low = jnp.where(jnp.right_shift(key, 16) == thr_hi, (key & 0xFFFF) + half_lo, half_lo)
        hi_ref[pl.ds(r0, COUNT_ROWS), :] = low.astype(hi_ref.dtype)
        return carry

    lax.fori_loop(0, n_rows // COUNT_ROWS, lower_plane, 0)

    lane = lax.broadcasted_iota(jnp.int32, (1, TQ), 1)
    enough_keys = i * TQ + lane + 1 >= ktop

    def unsettled(cge):
        return jnp.max(jnp.where(enough_keys & (cge != ktop), 1.0, 0.0))

    def lower_cond(carry):
        return (carry[0] < 32) & (carry[4] > 0.0)

    def lower_body(carry):
        it, tb, cge, crej, _ = carry
        for _ in range(LOWER_STEPS):
            cand_b = tb | jnp.left_shift(jnp.int32(1), 31 - it)
            cnt = above + count_upper((cand_b & 0xFFFF) + half_lo)
            tb, cge, crej = step(tb, cge, crej, cand_b, cnt)
            it = it + 1
        return it, tb, cge, crej, unsettled(cge)

    _, tb, cge, crej, _ = lax.while_loop(lower_cond, lower_body,
                                         (jnp.int32(16), tb, cge, crej, unsettled(cge)))
    thr = tb ^ jnp.int32(INT_MIN)
    has_thr = tb != 0

    need = ktop - crej
    any_tie = jnp.max(jnp.where(has_thr & (cge > ktop), 1.0, 0.0)) > 0.0

    @pl.when(any_tie)
    def _demote_surplus_ties():
        def body(it, p):
            cand = p | jnp.left_shift(jnp.int32(1), idx_bits - 1 - it)
            cnt = count(lambda blk, idx: (blk == thr) & (idx < cand))
            return jnp.where(cnt < need, cand, p)
        tie_last = lax.fori_loop(0, idx_bits, body, zeros_q)
        blk_iota = lax.broadcasted_iota(jnp.int32, (COUNT_ROWS, TQ), 0)

        def demote(r, carry):
            r0 = pl.multiple_of(r * COUNT_ROWS, COUNT_ROWS)
            blk = sc_ref[pl.ds(r0, COUNT_ROWS), :]
            surplus = has_thr & (blk == thr) & ((r0 + blk_iota) > tie_last)
            sc_ref[pl.ds(r0, COUNT_ROWS), :] = jnp.where(surplus, thr - 1, blk)
            return carry
        lax.fori_loop(0, n_rows // COUNT_ROWS, demote, 0)

    thr_ge = jnp.where(has_thr, thr, jnp.int32(INT_MIN + 1))

    pair_rows = lax.broadcasted_iota(jnp.int32, (2 * HEAD_DIM, TQ), 0)
    for h in range(N_HEADS):
        pair = qT_ref[(h // 2) * 2 * HEAD_DIM:(h // 2 + 1) * 2 * HEAD_DIM, :].astype(_F32)
        mine = (pair_rows >= HEAD_DIM) if h % 2 else (pair_rows < HEAD_DIM)
        qpad_ref[h] = jnp.where(mine, pair, 0.0).astype(qpad_ref.dtype)
    m_ref[...] = jnp.full(m_ref.shape, NEG_BIG, _F32)
    acc_ref[...] = jnp.zeros(acc_ref.shape, _F32)
    alpha_ref[...] = jnp.ones(alpha_ref.shape, _F32)
    for h in range(N_HEADS):
        p_refs[h][...] = jnp.zeros(p_refs[h].shape, p_refs[h].dtype)

    def apply_pv(kb):
        for h in range(N_HEADS):
            pv = jnp.dot(vT_ref[kb, h], p_refs[h][0], preferred_element_type=_F32)
            acc_ref[h] = alpha_ref[h:h + 1, :] * acc_ref[h] + pv

    def flash_body(kb, carry):
        apply_pv(jnp.maximum(kb - 1, 0))
        k0 = pl.multiple_of(kb * TKF, TKF)
        bias_ref[...] = jnp.where(sc_ref[pl.ds(k0, TKF), :] >= thr_ge, 0.0, NEG_BIG)
        for h in range(N_HEADS):
            for r in range(TKF // QK_ROWS):
                kpair = k_ref[pl.ds(k0 + r * QK_ROWS, QK_ROWS),
                              (h // 2) * 2 * HEAD_DIM:(h // 2 + 1) * 2 * HEAD_DIM]
                s_refs[h][0, r * QK_ROWS:(r + 1) * QK_ROWS, :] = jnp.dot(
                    kpair, qpad_ref[h], preferred_element_type=_F32)
        for h in range(N_HEADS):
            col_max = [None] * 4
            for j in range(TKF // SUBLANES):
                rows = slice(j * SUBLANES, (j + 1) * SUBLANES)
                slab = s_refs[h][0, rows, :] + bias_ref[rows, :]
                c = j % len(col_max)
                col_max[c] = slab if col_max[c] is None else jnp.maximum(col_max[c], slab)
            tile_max = jnp.maximum(jnp.maximum(col_max[0], col_max[1]),
                                   jnp.maximum(col_max[2], col_max[3]))
            m_old = m_ref[h:h + 1, :]
            m_new = jnp.maximum(m_old, jnp.max(tile_max, axis=0, keepdims=True))
            alpha_ref[h:h + 1, :] = jnp.exp2(m_old - m_new)
            m_ref[h:h + 1, :] = m_new
        for h in range(N_HEADS):
            p_refs[h][0] = jnp.exp2(s_refs[h][0] + bias_ref[...]
                                    - m_ref[h:h + 1, :]).astype(p_refs[h].dtype)
        return carry

    lax.fori_loop(0, n_pairs, flash_body, 0)
    apply_pv(n_pairs - 1)

    for h in range(N_HEADS):
        on_ref[h * HEAD_DIM:(h + 1) * HEAD_DIM, :] = (
            acc_ref[h, 0:HEAD_DIM, :] / acc_ref[h, HEAD_DIM:HEAD_DIM + 1, :])
    o_ref[...] = on_ref[...].T.astype(o_ref.dtype)


def _attention(qT, iqT, vT, wT, k, ik, ktop):
    B, attn_w, S = qT.shape
    idx_w = iqT.shape[1]
    idx_bits = max(1, math.ceil(math.log2(S)))
    grid = (B, S // TQ)
    return pl.pallas_call(
        functools.partial(_attn_kernel, ktop=ktop, idx_bits=idx_bits),
        grid=grid,
        in_specs=[
            pl.BlockSpec((None, idx_w, TQ), lambda b, i: (b, 0, i)),
            pl.BlockSpec((None, IDX_HEADS, TQ), lambda b, i: (b, 0, i)),
            pl.BlockSpec((None, attn_w, TQ), lambda b, i: (b, 0, i)),
            _resident((None, S, IDX_DIM), lambda b, i: (b, 0, 0)),
            _resident((None, S, attn_w), lambda b, i: (b, 0, 0)),
            _resident((None, S // TKF, N_HEADS, V_ROWS, TKF), lambda b, i: (b, 0, 0, 0, 0)),
        ],
        out_specs=pl.BlockSpec((None, TQ, attn_w), lambda b, i: (b, i, 0)),
        out_shape=jax.ShapeDtypeStruct((B, S, attn_w), MXU_DTYPE),
        scratch_shapes=[
            pltpu.VMEM((S, TQ), jnp.int32),
            pltpu.VMEM((S, TQ), jnp.int16),
            pltpu.VMEM((N_HEADS, 2 * HEAD_DIM, TQ), MXU_DTYPE),
            pltpu.VMEM((N_HEADS, V_ROWS, TQ), _F32),
            pltpu.VMEM((N_HEADS, TQ), _F32),
            pltpu.VMEM((N_HEADS, TQ), _F32),
            pltpu.VMEM((TKF, TQ), _F32),
            pltpu.VMEM((attn_w, TQ), _F32),
        ] + [pltpu.VMEM((1, TKF, TQ), _F32)] * N_HEADS
          + [pltpu.VMEM((1, TKF, TQ), MXU_DTYPE)] * N_HEADS,
        compiler_params=_params(),
        name="topk_attention",
    )(iqT, wT, qT, ik, k, vT)


def _mixer_kernel(x_ref, halo_ref, o_ref, g_ref, wpool_ref, wgate_ref, poolw_ref, pscale_ref,
                  wpp_ref, wap_ref, wout_ref, h_ref, u_ref, *, tm):
    i = pl.program_id(1)
    D = x_ref.shape[-1]
    group_dim = wpool_ref.shape[1] // POOL_GROUPS
    g = g_ref[...]
    x = x_ref[...]
    xn = _rmsnorm(x, g).astype(MXU_DTYPE)
    hn = _rmsnorm(halo_ref[...], g).astype(MXU_DTYPE)
    u_halo = jnp.dot(hn, wpool_ref[...], preferred_element_type=_F32)
    u_ref[0:POOL_HALO, :] = jnp.where(i > 0, u_halo, 0.0)
    u_ref[POOL_HALO:POOL_HALO + tm, :] = jnp.dot(xn, wpool_ref[...], preferred_element_type=_F32)

    t = i * tm + lax.broadcasted_iota(jnp.int32, (tm, group_dim), 0)
    mixed = []
    for gi, w in enumerate(POOL_WINDOWS):
        lanes = slice(gi * group_dim, (gi + 1) * group_dim)
        cur = u_ref[POOL_HALO:POOL_HALO + tm, lanes]
        sums = cur
        for j in range(1, w):
            sums = sums + u_ref[POOL_HALO - j:POOL_HALO - j + tm, lanes]
        cnt = jnp.minimum(t + 1, w).astype(_F32)
        pooled = sums / cnt - cur
        mixed.append(jnp.dot(pooled.astype(MXU_DTYPE), poolw_ref[gi], preferred_element_type=_F32))
    mixed = jnp.concatenate(mixed, axis=-1) * pscale_ref[...]
    y_pool = jnp.dot(mixed.astype(MXU_DTYPE), wpp_ref[...], preferred_element_type=_F32)
    y_attn = jnp.dot(o_ref[...], wap_ref[...], preferred_element_type=_F32)
    gates = jax.nn.sigmoid(jnp.dot(xn, wgate_ref[...], preferred_element_type=_F32))
    merged = gates[:, :D] * y_pool + gates[:, D:] * y_attn
    h_ref[...] = x + jnp.dot(merged.astype(MXU_DTYPE), wout_ref[...], preferred_element_type=_F32)


def _mixer_out(x, o, g, wpool, wgate, poolw, pscale, wpp, wap, wout, tm):
    B, S, D = x.shape
    pool_w = wpool.shape[1]
    attn_w = o.shape[-1]
    grid = (B, S // tm)
    const2 = lambda b, i: (0, 0)
    halo_blocks = tm // POOL_HALO
    return pl.pallas_call(
        functools.partial(_mixer_kernel, tm=tm),
        grid=grid,
        in_specs=[
            pl.BlockSpec((None, tm, D), lambda b, i: (b, i, 0)),
            pl.BlockSpec((None, POOL_HALO, D),
                         lambda b, i: (b, jnp.maximum(i * halo_blocks - 1, 0), 0)),
            pl.BlockSpec((None, tm, attn_w), lambda b, i: (b, i, 0)),
            pl.BlockSpec((1, D), const2),
            pl.BlockSpec(wpool.shape, const2),
            pl.BlockSpec(wgate.shape, const2),
            pl.BlockSpec(poolw.shape, lambda b, i: (0, 0, 0)),
            pl.BlockSpec((1, pool_w), const2),
            pl.BlockSpec(wpp.shape, const2),
            pl.BlockSpec(wap.shape, const2),
            pl.BlockSpec(wout.shape, const2),
        ],
        out_specs=pl.BlockSpec((None, tm, D), lambda b, i: (b, i, 0)),
        out_shape=jax.ShapeDtypeStruct((B, S, D), _F32),
        scratch_shapes=[pltpu.VMEM((POOL_HALO + tm, pool_w), _F32)],
        compiler_params=_params(),
        name="mixer_out",
    )(x, x, o, g, wpool, wgate, poolw, pscale, wpp, wap, wout)


def _ffn_kernel(h_ref, halo_ref, g_ref, wup_ref, cw_ref, cb_ref, wd_ref, gf_ref, out_ref,
                hn_ref, act_ref, ua0_ref, ua1_ref, ub0_ref, ub1_ref, *, tm, d_ff):
    ua_refs, ub_refs = (ua0_ref, ua1_ref), (ub0_ref, ub1_ref)
    i = pl.program_id(1)
    g = g_ref[...]
    h = h_ref[...]
    hn_ref[0:FFN_HALO, :] = _rmsnorm(halo_ref[...], g).astype(MXU_DTYPE)
    hn_ref[FFN_HALO:FFN_HALO + tm, :] = _rmsnorm(h, g).astype(MXU_DTYPE)
    first = i == 0

    def conv(u_ref, up, cw, cb):
        u_ref[FFN_HALO:FFN_HALO + tm, :] = up[FFN_HALO:, :]
        u_ref[0:FFN_HALO, :] = jnp.where(first, 0.0, up[0:FFN_HALO, :])
        out = cb
        for j in range(CONV_WIDTH):
            off = FFN_HALO - (CONV_WIDTH - 1) + j
            out = out + u_ref[off:off + tm, :] * cw[j:j + 1, :]
        return out

    for c in range(d_ff // FF_CHUNK):
        hn = hn_ref[...]
        cols_a = slice(c * FF_CHUNK, (c + 1) * FF_CHUNK)
        cols_b = slice(d_ff + c * FF_CHUNK, d_ff + (c + 1) * FF_CHUNK)
        a = conv(ua_refs[c % 2], jnp.dot(hn, wup_ref[:, cols_a], preferred_element_type=_F32),
                 cw_ref[:, cols_a], cb_ref[:, cols_a])
        b = conv(ub_refs[c % 2], jnp.dot(hn, wup_ref[:, cols_b], preferred_element_type=_F32),
                 cw_ref[:, cols_b], cb_ref[:, cols_b])
        act_ref[:, cols_a] = (a * jax.nn.sigmoid(a) * b).astype(MXU_DTYPE)
    down = jnp.dot(act_ref[...], wd_ref[...], preferred_element_type=_F32)
    out_ref[...] = _rmsnorm(h + down, gf_ref[...])


def _conv_ffn(h, g, wup, cw, cb, wd, gf, tm):
    B, S, D = h.shape
    d_ff = wd.shape[0]
    grid = (B, S // tm)
    const2 = lambda b, i: (0, 0)
    halo_blocks = tm // FFN_HALO
    return pl.pallas_call(
        functools.partial(_ffn_kernel, tm=tm, d_ff=d_ff),
        grid=grid,
        in_specs=[
            pl.BlockSpec((None, tm, D), lambda b, i: (b, i, 0)),
            pl.BlockSpec((None, FFN_HALO, D),
                         lambda b, i: (b, jnp.maximum(i * halo_blocks - 1, 0), 0)),
            pl.BlockSpec((1, D), const2),
            _resident(wup.shape, const2),
            pl.BlockSpec(cw.shape, const2),
            pl.BlockSpec(cb.shape, const2),
            _resident(wd.shape, const2),
            pl.BlockSpec((1, D), const2),
        ],
        out_specs=pl.BlockSpec((None, tm, D), lambda b, i: (b, i, 0)),
        out_shape=jax.ShapeDtypeStruct((B, S, D), _F32),
        scratch_shapes=[
            pltpu.VMEM((FFN_HALO + tm, D), MXU_DTYPE),
            pltpu.VMEM((tm, d_ff), MXU_DTYPE),
        ] + [pltpu.VMEM((FFN_HALO + tm, FF_CHUNK), _F32)] * 4,
        compiler_params=_params(),
        name="conv_ffn",
    )(h, h, g, wup, cw, cb, wd, gf)


def _rope_tables(S):
    half = ROT_DIM // 2
    inv_freq = 1.0 / (ROPE_THETA ** (jnp.arange(half, dtype=_F32) * 2.0 / ROT_DIM))
    ang = jnp.arange(S, dtype=_F32)[:, None] * inv_freq[None, :]
    cos, sin = jnp.cos(ang), jnp.sin(ang)
    zeros = jnp.zeros((S, HEAD_DIM - ROT_DIM), _F32)
    zh = jnp.zeros((S, half), _F32)
    ra = jnp.concatenate([cos, cos, jnp.ones_like(zeros)], axis=1)
    rb = jnp.concatenate([zh, sin, zeros], axis=1)
    rc = jnp.concatenate([-sin, zh, zeros], axis=1)
    rep = LANES // HEAD_DIM
    tile = lambda a: jnp.tile(a, (1, rep))
    return cos.T, sin.T, tile(ra), tile(rb), tile(rc)


def _layer(h, norm_mix_g, w_in, pool_w, pool_scale, w_pool_proj, w_attn_proj, w_out,
           norm_ffn_g, w_up, conv_w, conv_b, w_down, norm_out_g, tables):
    B, S, D = h.shape
    pool_width = D // 2
    attn_w = N_HEADS * HEAD_DIM
    idx_w = IDX_HEADS * IDX_DIM
    d_ff = w_down.shape[0]
    ktop = min(TOPK_MAX, S // 4)
    c0 = pool_width
    cq, ck, cv = c0, c0 + attn_w, c0 + 2 * attn_w
    ciq = c0 + 3 * attn_w
    cik = ciq + idx_w
    ciw = cik + IDX_DIM
    cg = ciw + IDX_HEADS
    cast = lambda a: a.astype(MXU_DTYPE)

    wrow = cast(jnp.concatenate(
        [w_in[:, ck:cv], w_in[:, cik:ciw], jnp.zeros((D, LANES - IDX_DIM), w_in.dtype)], axis=1))
    wt = cast(jnp.concatenate(
        [w_in[:, cq:ck], w_in[:, ciq:cik], w_in[:, cv:ciq], w_in[:, ciw:cg],
         jnp.zeros((D, 16 - IDX_HEADS), w_in.dtype)], axis=1).T)
    row1 = lambda a: a.reshape(1, -1)

    tm = TKF
    qT, iqT, vT, wT, k, ik = _projection(h, row1(norm_mix_g), wrow, wt, *tables, tm=tm)
    o = _attention(qT, iqT, vT, wT, k, ik, ktop)
    h1 = _mixer_out(h, o, row1(norm_mix_g), cast(w_in[:, :c0]), cast(w_in[:, cg:]),
                    cast(pool_w), row1(pool_scale), cast(w_pool_proj), cast(w_attn_proj),
                    cast(w_out), tm=tm)

    assert d_ff % FF_CHUNK == 0
    return _conv_ffn(h1, row1(norm_ffn_g), cast(w_up), conv_w, row1(conv_b), cast(w_down),
                     row1(norm_out_g), tm=tm)


def kernel(x, norm_mix_g, w_in, pool_w, pool_scale, w_pool_proj, w_attn_proj, w_out, norm_ffn_g,
           w_up, conv_w, conv_b, w_down, norm_final_g):
    depth = w_in.shape[0]
    assert depth == 1, "the final RMSNorm is fused into the last layer's ConvFFN kernel"
    S = x.shape[1]
    assert S % TKF == 0 and TQ == TK and TKF % TK == 0
    tables = _rope_tables(S)
    return _layer(x, norm_mix_g[0], w_in[0], pool_w[0], pool_scale[0], w_pool_proj[0],
                  w_attn_proj[0], w_out[0], norm_ffn_g[0], w_up[0], conv_w[0], conv_b[0],
                  w_down[0], norm_final_g, tables)
```
